```python
import math
import jax, jax.numpy as jnp
from jax import lax
import numpy as np


D_MODEL = 1024
BATCH = 2
SEQ = 16384
DEPTH = 4
DEC_BATCH = 16
DEC_SEQ = 2048
PAST_LEN = 128

MLA_HEADS = D_MODEL // 128
QK_NOPE_DIM = 64
QK_ROPE_DIM = 32
V_HEAD_DIM = 64
Q_LORA_RANK = D_MODEL // 4
KV_LORA_RANK = D_MODEL // 8
ROPE_THETA = 10000.0
Q_BLOCK = 128
MLA_OUT = MLA_HEADS * V_HEAD_DIM
POOL_GROUPS = 4
POOL_GROUP_DIM = D_MODEL // 16
POOL_WINDOWS = (2, 4, 8, 16)
POOL_WIDTH = POOL_GROUPS * POOL_GROUP_DIM
MLSTM_HEADS = 4
MLSTM_HEAD_DIM = D_MODEL // 16
MLSTM_WIDTH = MLSTM_HEADS * MLSTM_HEAD_DIM
MLSTM_CHUNK = 128
N_GATES = 4 * MLSTM_HEADS
NEG_BIG = -1e30
MIX_WIDTH = MLA_OUT + POOL_WIDTH + MLSTM_WIDTH
IN_WIDTHS = (Q_LORA_RANK, KV_LORA_RANK, QK_ROPE_DIM, POOL_WIDTH,
             MLSTM_WIDTH, MLSTM_WIDTH, MLSTM_WIDTH, MLSTM_WIDTH, N_GATES)
IN_WIDTH = sum(IN_WIDTHS)
IN_SPLITS = tuple(int(v) for v in np.cumsum(IN_WIDTHS)[:-1])
D_FF = -(-8 * D_MODEL // (3 * 256)) * 256
DEEPNORM_ALPHA = (2 * DEPTH) ** 0.25
DEEPNORM_BETA = (8 * DEPTH) ** -0.25
LN_EPS = 1e-5

kernel_name = 'hymba_mla_pool_mlstm_deepnorm_encoder'


def layer_norm(x, g, b):
    xf = x.astype(jnp.float32)
    mu = xf.mean(-1, keepdims=True)
    var = jnp.square(xf - mu).mean(-1, keepdims=True)
    return ((xf - mu) * lax.rsqrt(var + LN_EPS) * g + b).astype(x.dtype)


def rms_norm(x, g):
    xf = x.astype(jnp.float32)
    return (xf * lax.rsqrt(jnp.square(xf).mean(-1, keepdims=True) + LN_EPS) * g).astype(x.dtype)


def rope_tables(seq_len):
    inv = 1.0 / (ROPE_THETA ** (jnp.arange(0, QK_ROPE_DIM, 2, dtype=jnp.float32) / QK_ROPE_DIM))
    ang = jnp.arange(seq_len, dtype=jnp.float32)[:, None] * inv[None, :]
    return jnp.cos(ang), jnp.sin(ang)


def apply_rope(x, cos, sin):
    xf = x.astype(jnp.float32)
    half = QK_ROPE_DIM // 2
    x1, x2 = xf[..., :half], xf[..., half:]
    return jnp.concatenate([x1 * cos - x2 * sin, x2 * cos + x1 * sin], axis=-1).astype(x.dtype)


def mla_mix(c_q, c_kv, k_rope, q_norm_g, w_uq, kv_norm_g, w_ukv):
    B, S, _ = c_q.shape
    q = jnp.einsum('bsr,rn->bsn', rms_norm(c_q, q_norm_g), w_uq).reshape(
        B, S, MLA_HEADS, QK_NOPE_DIM + QK_ROPE_DIM)
    kv = jnp.einsum('bsr,rn->bsn', rms_norm(c_kv, kv_norm_g), w_ukv).reshape(
        B, S, MLA_HEADS, QK_NOPE_DIM + V_HEAD_DIM)
    k_nope, v = kv[..., :QK_NOPE_DIM], kv[..., QK_NOPE_DIM:]
    cos, sin = rope_tables(S)
    q_rope = apply_rope(q[..., QK_NOPE_DIM:], cos[:, None, :], sin[:, None, :])
    k_rope = apply_rope(k_rope, cos, sin)
    q = jnp.concatenate([q[..., :QK_NOPE_DIM], q_rope], axis=-1)
    k = jnp.concatenate(
        [k_nope, jnp.broadcast_to(k_rope[:, :, None, :], (B, S, MLA_HEADS, QK_ROPE_DIM))], axis=-1)
    scale = (QK_NOPE_DIM + QK_ROPE_DIM) ** -0.5
    q_blocks = q.reshape(B, S // Q_BLOCK, Q_BLOCK, MLA_HEADS, -1).transpose(1, 0, 2, 3, 4)

    def attend(q_blk):
        s = jnp.einsum('bqhd,bkhd->bhqk', q_blk, k, preferred_element_type=jnp.float32) * scale
        p = jax.nn.softmax(s, axis=-1).astype(v.dtype)
        return jnp.einsum('bhqk,bkhd->bqhd', p, v)

    o = lax.map(attend, q_blocks)
    return o.transpose(1, 0, 2, 3, 4).reshape(B, S, MLA_OUT)


def pool_mix(xp, w_pool, pool_scale):
    B, S, _ = xp.shape
    xf = xp.astype(jnp.float32)
    cs = jnp.concatenate([jnp.zeros((B, 1, POOL_WIDTH), jnp.float32), jnp.cumsum(xf, axis=1)], axis=1)
    t = jnp.arange(S)
    outs = []
    for g, w in enumerate(POOL_WINDOWS):
        lo = jnp.clip(t - w // 2, 0, S)
        hi = jnp.clip(t + w // 2, 0, S)
        sl = slice(g * POOL_GROUP_DIM, (g + 1) * POOL_GROUP_DIM)
        csg = cs[:, :, sl]
        win_sum = jnp.take(csg, hi, axis=1) - jnp.take(csg, lo, axis=1)
        cnt = (hi - lo).astype(jnp.float32)[None, :, None]
        outs.append(win_sum / cnt - xf[:, :, sl])
    y = jnp.stack(outs, axis=2).astype(xp.dtype)
    y = jnp.einsum('bsgc,gcd->bsgd', y, w_pool).reshape(B, S, POOL_WIDTH)
    return y * pool_scale


def mlstm_chunk_scan(q, k, v, i_pre, f_pre):
    B, H, S, D = q.shape
    L = MLSTM_CHUNK
    NC = S // L

    def to_chunks(a):
        return jnp.moveaxis(a.reshape(B, H, NC, L, *a.shape[3:]), 2, 0)

    logf = jax.nn.log_sigmoid(f_pre)
    lower = jnp.tril(jnp.ones((L, L), dtype=bool))

    def step(carry, inp):
        C, n, m = carry
        qc, kc, vc, ic, fc = inp
        b = jnp.cumsum(fc, axis=-1)
        d_mat = jnp.where(lower, b[..., :, None] - b[..., None, :] + ic[..., None, :], NEG_BIG)
        m_inter = b + m[..., None]
        m_j = jnp.maximum(m_inter, d_mat.max(-1))
        w_intra = jnp.exp(d_mat - m_j[..., None])
        w_inter = jnp.exp(m_inter - m_j)
        scores = jnp.einsum('bhjd,bhsd->bhjs', qc, kc) * w_intra
        num = (w_inter[..., None] * jnp.einsum('bhjk,bhkv->bhjv', qc, C)
               + jnp.einsum('bhjs,bhsv->bhjv', scores, vc))
        den = w_inter * jnp.einsum('bhjk,bhk->bhj', qc, n) + scores.sum(-1)
        h = num / jnp.maximum(jnp.abs(den), jnp.exp(-m_j))[..., None]
        b_last = b[..., -1]
        g = b_last[..., None] - b + ic
        m_new = jnp.maximum(b_last + m, g.max(-1))
        decay = jnp.exp(b_last + m - m_new)
        w_s = jnp.exp(g - m_new[..., None])
        C_new = decay[..., None, None] * C + jnp.einsum('bhs,bhsk,bhsv->bhkv', w_s, kc, vc)
        n_new = decay[..., None] * n + jnp.einsum('bhs,bhsk->bhk', w_s, kc)
        return (C_new, n_new, m_new), h

    init = (jnp.zeros((B, H, D, D), jnp.float32), jnp.zeros((B, H, D), jnp.float32),
            jnp.zeros((B, H), jnp.float32))
    _, h = lax.scan(step, init, (to_chunks(q), to_chunks(k), to_chunks(v),
                                 to_chunks(i_pre), to_chunks(logf)))
    return jnp.moveaxis(h, 0, 2).reshape(B, H, S, D)


def mlstm_mix(q, k, v, o_pre, gates, gate_bias, norm_g):
    B, S, _ = q.shape

    def heads(a):
        return a.astype(jnp.float32).reshape(B, S, MLSTM_HEADS, MLSTM_HEAD_DIM).transpose(0, 2, 1, 3)

    qh, kh, vh = heads(q), heads(k) * MLSTM_HEAD_DIM ** -0.5, heads(v)
    g = (gates.astype(jnp.float32) + gate_bias.astype(jnp.float32)).reshape(
        B, S, 4, MLSTM_HEADS).transpose(2, 0, 3, 1)
    h_fwd = mlstm_chunk_scan(qh, kh, vh, g[0], g[1])
    flip = lambda a: jnp.flip(a, axis=2)
    h_bwd = flip(mlstm_chunk_scan(flip(qh), flip(kh), flip(vh), flip(g[2]), flip(g[3])))
    h = h_fwd + h_bwd
    mu = h.mean(-1, keepdims=True)
    var = jnp.square(h - mu).mean(-1, keepdims=True)
    h = ((h - mu) * lax.rsqrt(var + LN_EPS)).transpose(0, 2, 1, 3).reshape(B, S, MLSTM_WIDTH) * norm_g
    return (jax.nn.sigmoid(o_pre.astype(jnp.float32)) * h).astype(q.dtype)


def encoder_layer(x, w_in, q_norm_g, w_uq, kv_norm_g, w_ukv, w_pool, pool_scale,
                  mlstm_gate_bias, mlstm_norm_g, w_out, ln1_g, ln1_b,
                  w_gate, w_up, w_down, ln2_g, ln2_b):
    proj = jnp.einsum('bsd,dn->bsn', x, w_in)
    c_q, c_kv, k_rope, x_pool, q_m, k_m, v_m, o_m, gates = jnp.split(proj, IN_SPLITS, axis=-1)
    y_mla = mla_mix(c_q, c_kv, k_rope, q_norm_g, w_uq, kv_norm_g, w_ukv)
    y_pool = pool_mix(x_pool, w_pool, pool_scale).astype(x.dtype)
    y_mlstm = mlstm_mix(q_m, k_m, v_m, o_m, gates, mlstm_gate_bias, mlstm_norm_g)
    mix = jnp.einsum('bsm,md->bsd', jnp.concatenate([y_mla, y_pool, y_mlstm], axis=-1), w_out)
    x = layer_norm(DEEPNORM_ALPHA * x + mix, ln1_g, ln1_b)
    hid = jax.nn.silu(jnp.einsum('bsd,df->bsf', x, w_gate)) * jnp.einsum('bsd,df->bsf', x, w_up)
    ffn = jnp.einsum('bsf,fd->bsd', hid, w_down)
    return layer_norm(DEEPNORM_ALPHA * x + ffn, ln2_g, ln2_b)


def setup_inputs(seed: int = 0) -> dict:
    key = jax.random.key(seed)
    ks = jax.random.split(key, 24)

    def nrm(k, shape, scale):
        return jax.random.normal(k, shape, jnp.float32) * scale

    x_prompt = nrm(ks[0], (BATCH, SEQ, D_MODEL), 1.0)
    x_sample = nrm(ks[1], (DEC_BATCH, DEC_SEQ, D_MODEL), 1.0)
    ln_in_g = 1.0 + nrm(ks[2], (D_MODEL,), 0.02)
    ln_in_b = nrm(ks[3], (D_MODEL,), 0.02)
    w_in = nrm(ks[4], (DEPTH, D_MODEL, IN_WIDTH), D_MODEL ** -0.5)
    q_norm_g = 1.0 + nrm(ks[5], (DEPTH, Q_LORA_RANK), 0.02)
    w_uq = nrm(ks[6], (DEPTH, Q_LORA_RANK, MLA_HEADS * (QK_NOPE_DIM + QK_ROPE_DIM)), Q_LORA_RANK ** -0.5)
    kv_norm_g = 1.0 + nrm(ks[7], (DEPTH, KV_LORA_RANK), 0.02)
    w_ukv = nrm(ks[8], (DEPTH, KV_LORA_RANK, MLA_HEADS * (QK_NOPE_DIM + V_HEAD_DIM)), KV_LORA_RANK ** -0.5)
    w_pool = nrm(ks[9], (DEPTH, POOL_GROUPS, POOL_GROUP_DIM, POOL_GROUP_DIM), POOL_GROUP_DIM ** -0.5)
    pool_scale = 1.0 + nrm(ks[10], (DEPTH, POOL_WIDTH), 0.02)
    i_bias = nrm(ks[11], (DEPTH, 2, MLSTM_HEADS), 0.1)
    f_bias = jnp.linspace(3.0, 6.0, MLSTM_HEADS, dtype=jnp.float32)[None, None, :] + nrm(
        ks[12], (DEPTH, 2, MLSTM_HEADS), 0.1)
    mlstm_gate_bias = jnp.stack([i_bias[:, 0], f_bias[:, 0], i_bias[:, 1], f_bias[:, 1]],
                                axis=1).reshape(DEPTH, N_GATES)
    mlstm_norm_g = 1.0 + nrm(ks[13], (DEPTH, MLSTM_WIDTH), 0.02)
    w_out = nrm(ks[14], (DEPTH, MIX_WIDTH, D_MODEL), MIX_WIDTH ** -0.5 * DEEPNORM_BETA)
    ln1_g = 1.0 + nrm(ks[15], (DEPTH, D_MODEL), 0.02)
    ln1_b = nrm(ks[16], (DEPTH, D_MODEL), 0.02)
    w_gate = nrm(ks[17], (DEPTH, D_MODEL, D_FF), D_MODEL ** -0.5)
    w_up = nrm(ks[18], (DEPTH, D_MODEL, D_FF), D_MODEL ** -0.5)
    w_down = nrm(ks[19], (DEPTH, D_FF, D_MODEL), D_FF ** -0.5 * DEEPNORM_BETA)
    ln2_g = 1.0 + nrm(ks[20], (DEPTH, D_MODEL), 0.02)
    ln2_b = nrm(ks[21], (DEPTH, D_MODEL), 0.02)
    return {'x_prompt': x_prompt, 'x_sample': x_sample, 'ln_in_g': ln_in_g, 'ln_in_b': ln_in_b,
            'w_in': w_in, 'q_norm_g': q_norm_g, 'w_uq': w_uq, 'kv_norm_g': kv_norm_g, 'w_ukv': w_ukv,
            'w_pool': w_pool, 'pool_scale': pool_scale, 'mlstm_gate_bias': mlstm_gate_bias,
            'mlstm_norm_g': mlstm_norm_g, 'w_out': w_out, 'ln1_g': ln1_g, 'ln1_b': ln1_b,
            'w_gate': w_gate, 'w_up': w_up, 'w_down': w_down, 'ln2_g': ln2_g, 'ln2_b': ln2_b}


def reference(x_prompt, x_sample, ln_in_g, ln_in_b, w_in, q_norm_g, w_uq, kv_norm_g, w_ukv,
              w_pool, pool_scale, mlstm_gate_bias, mlstm_norm_g, w_out, ln1_g, ln1_b,
              w_gate, w_up, w_down, ln2_g, ln2_b):
    def trunk(x):
        x = layer_norm(x, ln_in_g, ln_in_b)
        for l in range(DEPTH):
            x = encoder_layer(x, w_in[l], q_norm_g[l], w_uq[l], kv_norm_g[l], w_ukv[l],
                              w_pool[l], pool_scale[l], mlstm_gate_bias[l], mlstm_norm_g[l],
                              w_out[l], ln1_g[l], ln1_b[l], w_gate[l], w_up[l], w_down[l],
                              ln2_g[l], ln2_b[l])
        return x

    y_prompt = trunk(x_prompt)
    y_sample = trunk(x_sample)
    return (y_prompt, y_sample)
```

```python
import functools
import math

import jax
import jax.numpy as jnp
import numpy as np
from jax import lax
from jax.experimental import pallas as pl
from jax.experimental.pallas import tpu as pltpu

F32 = jnp.float32
BF16 = jnp.bfloat16

D_MODEL = 1024
DEPTH = 4
N_HEADS = 8
D_NOPE = 64
D_ROPE = 32
D_V = 64
Q_RANK = 256
KV_RANK = 128
ROPE_THETA = 10000.0
D_HEAD_PAD = 128
POOL_WIDTH = 256
POOL_WINDOWS = (2, 4, 8, 16)
POOL_GROUP = 64
POOL_HALO = 8
ML_HEADS = 4
ML_DIM = 64
ML_WIDTH = 256
ML_CHUNK = 128
N_GATES = 16
NEG_BIG = -1e30
D_FF = 2816
ALPHA = (2 * DEPTH) ** 0.25
LN_EPS = 1e-5
LOG2E = 1.4426950408889634

VMEM_LIMIT = 56 * 1024 * 1024

_PA_CQ, _PA_CKV, _PA_KR, _PA_KRS, _PA_POOL, _PA_QM, _PA_VM, _PA_OM, _PA_END = (
    0, 256, 384, 512, 640, 896, 1152, 1408, 1664)

_NT = (((1,), (1,)), ((), ()))


def _layer_norm(x, g, b):
    mu = jnp.mean(x, axis=-1, keepdims=True)
    xc = x - mu
    var = jnp.mean(xc * xc, axis=-1, keepdims=True)
    return xc * lax.rsqrt(var + LN_EPS) * g + b


def _rms_norm(x, g):
    return x * lax.rsqrt(jnp.mean(x * x, axis=-1, keepdims=True) + LN_EPS) * g


def _in_proj_kernel(*refs, apply_ln):
    if apply_ln:
        (x_ref, lng_ref, lnb_ref, wa_ref, wt_ref, gb_ref, qg_ref, wq_ref, kvg_ref, wuk_ref, wuvt_ref,
         cq_ref, sq_ref, ck_ref, sk_ref,
         xn_ref, q_ref, k_ref, vt_ref, pool_ref, qm_ref, vm_ref, om_ref, kmt_ref, gt_ref) = refs
    else:
        (x_ref, wa_ref, wt_ref, gb_ref, qg_ref, wq_ref, kvg_ref, wuk_ref, wuvt_ref,
         cq_ref, sq_ref, ck_ref, sk_ref,
         q_ref, k_ref, vt_ref, pool_ref, qm_ref, vm_ref, om_ref, kmt_ref, gt_ref) = refs
    x = x_ref[...]
    if apply_ln:
        x = _layer_norm(x, lng_ref[...], lnb_ref[...])
        xn_ref[...] = x
    xb = x.astype(BF16)
    pa = jnp.dot(xb, wa_ref[...], preferred_element_type=F32)
    pt = lax.dot_general(wt_ref[...], xb, _NT, preferred_element_type=F32)
    kmt_ref[...] = pt[:ML_WIDTH].astype(BF16)
    gt_ref[...] = pt[ML_WIDTH:] + gb_ref[...]

    pool_ref[...] = pa[:, _PA_POOL:_PA_QM]
    qm_ref[...] = pa[:, _PA_QM:_PA_VM].astype(BF16)
    vm_ref[...] = pa[:, _PA_VM:_PA_OM].astype(BF16)
    om_ref[...] = pa[:, _PA_OM:_PA_END]

    cqn = _rms_norm(pa[:, _PA_CQ:_PA_CKV], qg_ref[...]).astype(BF16)
    q2 = jnp.dot(cqn, wq_ref[...], preferred_element_type=F32)
    cq, sq = cq_ref[...], sq_ref[...]
    hw = N_HEADS * D_HEAD_PAD
    for h in range(N_HEADS):
        lo = h * D_HEAD_PAD
        qh = q2[:, lo:lo + D_HEAD_PAD] * cq + q2[:, hw + lo:hw + lo + D_HEAD_PAD] * sq
        q_ref[:, lo:lo + D_HEAD_PAD] = qh.astype(BF16)

    ckvn = _rms_norm(pa[:, _PA_CKV:_PA_KR], kvg_ref[...]).astype(BF16)
    kn = jnp.dot(ckvn, wuk_ref[...], preferred_element_type=F32)
    kr = pa[:, _PA_KR:_PA_KRS] * ck_ref[...] + pa[:, _PA_KRS:_PA_POOL] * sk_ref[...]
    for h in range(N_HEADS):
        lo = h * D_HEAD_PAD
        k_ref[:, lo:lo + D_HEAD_PAD] = (kn[:, lo:lo + D_HEAD_PAD] + kr).astype(BF16)
    vt = lax.dot_general(wuvt_ref[...], ckvn, _NT, preferred_element_type=F32)
    vt_ref[...] = vt.astype(BF16)


def _const_spec(shape):
    nd = len(shape)
    return pl.BlockSpec(shape, lambda *_: (0,) * nd)


def _in_proj(x, lw, tabs, seq, *, ln=None, tm=256):
    t_tokens = x.shape[0]
    n_seq_tiles = seq // tm
    apply_ln = ln is not None
    row = lambda w: pl.BlockSpec((tm, w), lambda i: (i, 0))
    colt = lambda r: pl.BlockSpec((r, tm), lambda i: (0, i))
    tab = pl.BlockSpec((tm, D_HEAD_PAD), lambda i: (i % n_seq_tiles, 0))
    consts = [lw["wa"], lw["wt"], lw["gate_bias"], lw["q_norm_g"], lw["wq2"], lw["kv_norm_g"],
              lw["wuk"], lw["wuvt"]]
    in_specs = [row(D_MODEL)]
    args = [x]
    if apply_ln:
        in_specs += [_const_spec((1, D_MODEL))] * 2
        args += [ln[0], ln[1]]
    in_specs += [_const_spec(c.shape) for c in consts] + [tab] * 4
    args += consts + list(tabs)
    hw = N_HEADS * D_HEAD_PAD
    out_shape, out_specs = [], []
    if apply_ln:
        out_shape.append(jax.ShapeDtypeStruct((t_tokens, D_MODEL), F32))
        out_specs.append(row(D_MODEL))
    out_shape += [
        jax.ShapeDtypeStruct((t_tokens, hw), BF16),
        jax.ShapeDtypeStruct((t_tokens, hw), BF16),
        jax.ShapeDtypeStruct((N_HEADS * D_V, t_tokens), BF16),
        jax.ShapeDtypeStruct((t_tokens, POOL_WIDTH), F32),
        jax.ShapeDtypeStruct((t_tokens, ML_WIDTH), BF16),
        jax.ShapeDtypeStruct((t_tokens, ML_WIDTH), BF16),
        jax.ShapeDtypeStruct((t_tokens, ML_WIDTH), F32),
        jax.ShapeDtypeStruct((ML_WIDTH, t_tokens), BF16),
        jax.ShapeDtypeStruct((N_GATES, t_tokens), F32),
    ]
    out_specs += [row(hw), row(hw), colt(N_HEADS * D_V), row(POOL_WIDTH), row(ML_WIDTH), row(ML_WIDTH),
                  row(ML_WIDTH), colt(ML_WIDTH), colt(N_GATES)]
    return pl.pallas_call(
        functools.partial(_in_proj_kernel, apply_ln=apply_ln),
        grid=(t_tokens // tm,),
        in_specs=in_specs,
        out_specs=out_specs,
        out_shape=out_shape,
        compiler_params=pltpu.CompilerParams(dimension_semantics=("parallel",),
                                             vmem_limit_bytes=VMEM_LIMIT),
        name="in_proj_ln" if apply_ln else "in_proj",
    )(*args)


def _attention_kernel(q_ref, k_ref, vt_ref, o_ref, *, tk):
    q = q_ref[...]
    tq = q.shape[0]
    n_k = k_ref.shape[0] // tk

    def body(i, carry):
        m, l, acc = carry
        ks = pl.multiple_of(i * tk, tk)
        s = lax.dot_general(k_ref[pl.ds(ks, tk), :], q, _NT, preferred_element_type=F32)
        m_new = jnp.maximum(m, jnp.max(s, axis=0, keepdims=True))
        alpha = jnp.exp2(m - m_new)
        p = jnp.exp2(s - m_new)
        l = alpha * l + jnp.sum(p, axis=0, keepdims=True)
        pv = jnp.dot(vt_ref[:, pl.ds(ks, tk)], p.astype(BF16), preferred_element_type=F32)
        return m_new, l, alpha * acc + pv

    m0 = jnp.full((1, tq), NEG_BIG, F32)
    l0 = jnp.zeros((1, tq), F32)
    a0 = jnp.zeros((D_V, tq), F32)
    _, l, acc = lax.fori_loop(0, n_k, body, (m0, l0, a0))
    o_ref[...] = (acc / l).astype(BF16)


def _attention(q, k, vt, batch, seq, *, tq=256, tk=512):
    n_q = seq // tq
    t_tokens = batch * seq
    q3 = q.reshape(batch, seq, N_HEADS * D_HEAD_PAD)
    k3 = k.reshape(batch, seq, N_HEADS * D_HEAD_PAD)
    return pl.pallas_call(
        functools.partial(_attention_kernel, tk=tk),
        grid=(batch, N_HEADS, n_q),
        in_specs=[
            pl.BlockSpec((None, tq, D_HEAD_PAD), lambda b, h, i: (b, i, h)),
            pl.BlockSpec((None, seq, D_HEAD_PAD), lambda b, h, i: (b, 0, h)),
            pl.BlockSpec((D_V, seq), lambda b, h, i: (h, b)),
        ],
        out_specs=pl.BlockSpec((D_V, tq), lambda b, h, i: (h, b * n_q + i)),
        out_shape=jax.ShapeDtypeStruct((N_HEADS * D_V, t_tokens), BF16),
        compiler_params=pltpu.CompilerParams(
            dimension_semantics=("parallel", "parallel", "arbitrary"), vmem_limit_bytes=VMEM_LIMIT),
        name="attention",
    )(q3, k3, vt)


def _pool_kernel(x_ref, prev_ref, next_ref, w_ref, scale_ref, o_ref, *, seq):
    i = pl.program_id(1)
    ts = x_ref.shape[0]
    x = x_ref[...]
    prev = jnp.where(i == 0, 0.0, prev_ref[...])
    nxt = jnp.where(i == pl.num_programs(1) - 1, 0.0, next_ref[...])
    xe = jnp.concatenate([prev, x, nxt], axis=0)
    n = ts + 2 * POOL_HALO
    p2 = xe[0:n - 1] + xe[1:n]
    p4 = p2[0:n - 3] + p2[2:n - 1]
    p8 = p4[0:n - 7] + p4[4:n - 3]
    p16 = p8[0:n - 15] + p8[8:n - 7]
    sums = {2: p2, 4: p4, 8: p8, 16: p16}
    t = i * ts + lax.broadcasted_iota(jnp.int32, (ts, 1), 0)
    lane = lax.broadcasted_iota(jnp.int32, (1, POOL_WIDTH), 1)
    y = jnp.zeros((ts, POOL_WIDTH), F32)
    for g, w in enumerate(POOL_WINDOWS):
        start = POOL_HALO - w // 2
        win = sums[w][start:start + ts]
        cnt = (jnp.minimum(t + w // 2, seq) - jnp.maximum(t - w // 2, 0)).astype(F32)
        y = jnp.where(lane // POOL_GROUP == g, win / cnt - x, y)
    out = jnp.dot(y.astype(BF16), w_ref[...], preferred_element_type=F32) * scale_ref[...]
    o_ref[...] = out.astype(BF16)


def _pool(xp, w_bd, scale, batch, seq, *, ts=512):
    n_t = seq // ts
    hb = ts // POOL_HALO
    x3 = xp.reshape(batch, seq, POOL_WIDTH)
    out = pl.pallas_call(
        functools.partial(_pool_kernel, seq=seq),
        grid=(batch, n_t),
        in_specs=[
            pl.BlockSpec((None, ts, POOL_WIDTH), lambda b, i: (b, i, 0)),
            pl.BlockSpec((None, POOL_HALO, POOL_WIDTH), lambda b, i: (b, jnp.maximum(i * hb - 1, 0), 0)),
            pl.BlockSpec((None, POOL_HALO, POOL_WIDTH),
                         lambda b, i: (b, jnp.minimum((i + 1) * hb, seq // POOL_HALO - 1), 0)),
            _const_spec((POOL_WIDTH, POOL_WIDTH)),
            _const_spec((1, POOL_WIDTH)),
        ],
        out_specs=pl.BlockSpec((None, ts, POOL_WIDTH), lambda b, i: (b, i, 0)),
        out_shape=jax.ShapeDtypeStruct((batch, seq, POOL_WIDTH), BF16),
        compiler_params=pltpu.CompilerParams(dimension_semantics=("parallel", "parallel")),
        name="pool",
    )(x3, x3, x3, w_bd, scale)
    return out.reshape(batch * seq, POOL_WIDTH)


def _log_sigmoid(x):
    return -(jnp.maximum(-x, 0.0) + jnp.log1p(jnp.exp(-jnp.abs(x))))


def _mlstm_direction(q_ref, kt_ref, v_ref, g_ref, h_ref, c_ref, n_ref, m_ref, reverse):
    L = ML_CHUNK
    hi = lax.Precision.HIGHEST
    r = lax.broadcasted_iota(jnp.int32, (L, L), 0)
    c = lax.broadcasted_iota(jnp.int32, (L, L), 1)
    if reverse:
        inc = (c >= r)
    else:
        inc = (c <= r)
    inc_f = inc.astype(F32)
    g = g_ref[...]
    i_base = 8 if reverse else 0
    f_base = i_base + ML_HEADS
    logf = _log_sigmoid(g)
    b_rows = lax.dot_general(logf, inc_f, _NT, precision=hi, preferred_element_type=F32)
    q = q_ref[...]
    kt = kt_ref[...]
    v = v_ref[...]
    ones_b = jnp.ones((L, L), BF16)
    lane_q = lax.broadcasted_iota(jnp.int32, (L, ML_WIDTH), 1)
    lane = lax.broadcasted_iota(jnp.int32, (L, L), 1)
    last = 0 if reverse else L - 1
    c_bf = c_ref[...].astype(BF16)
    n_bf = n_ref[...].astype(BF16)
    outs = []
    for h in range(ML_HEADS):
        b_row = b_rows[f_base + h:f_base + h + 1]
        i_row = g[i_base + h:i_base + h + 1]
        b_col = lax.dot_general(inc_f, jnp.broadcast_to(logf[f_base + h:f_base + h + 1], (L, L)), _NT,
                                precision=hi, preferred_element_type=F32)
        m_old = m_ref[h:h + 1, :]
        dm = jnp.where(inc, (b_col - b_row) + i_row, NEG_BIG)
        m_inter = b_col + m_old
        m_j = jnp.maximum(m_inter, jnp.max(dm, axis=1, keepdims=True))
        w_intra = jnp.exp(dm - m_j)
        w_inter = jnp.exp(m_inter - m_j)
        qh = jnp.where(lane_q // ML_DIM == h, q, jnp.zeros_like(q))
        kth = kt[h * ML_DIM:(h + 1) * ML_DIM]
        scores = jnp.dot(qh, kt, preferred_element_type=F32) * w_intra
        sb = scores.astype(BF16)
        pair = (h // 2) * L
        v_pair = v[:, pair:pair + L]
        num = w_inter * jnp.dot(qh, c_bf, preferred_element_type=F32) \
            + jnp.dot(sb, v_pair, preferred_element_type=F32)
        den = w_inter * jnp.dot(qh, n_bf, preferred_element_type=F32) \
            + jnp.dot(sb, ones_b, preferred_element_type=F32)
        outs.append(num / jnp.maximum(jnp.abs(den), jnp.exp(-m_j)))
        b_last = b_col[last:last + 1]
        g_row = (b_last - b_row) + i_row
        m_new = jnp.maximum(b_last + m_old, jnp.max(g_row, axis=1, keepdims=True))
        decay = jnp.exp(b_last + m_old - m_new)
        w_s = jnp.exp(g_row - m_new)
        kw = (kth.astype(F32) * w_s).astype(BF16)
        rows = slice(h * ML_DIM, (h + 1) * ML_DIM)
        c_ref[rows, :] = decay * c_ref[rows, :] + jnp.dot(kw, v_pair, preferred_element_type=F32)
        n_ref[rows, :] = decay * n_ref[rows, :] + jnp.dot(kw, ones_b, preferred_element_type=F32)
        m_ref[h:h + 1, :] = m_new
    for p in range(ML_HEADS // 2):
        h_ref[:, p * L:(p + 1) * L] = jnp.where(lane < ML_DIM, outs[2 * p], outs[2 * p + 1])


def _mlstm_kernel(qf_ref, ktf_ref, vf_ref, gf_ref, qb_ref, ktb_ref, vb_ref, gb_ref,
                  hf_ref, hb_ref, c_ref, n_ref, m_ref):
    @pl.when(pl.program_id(1) == 0)
    def _():
        c_ref[...] = jnp.zeros_like(c_ref)
        n_ref[...] = jnp.zeros_like(n_ref)
        m_ref[...] = jnp.zeros_like(m_ref)

    _mlstm_direction(qf_ref, ktf_ref, vf_ref, gf_ref, hf_ref, c_ref.at[0], n_ref.at[0], m_ref.at[0], False)
    _mlstm_direction(qb_ref, ktb_ref, vb_ref, gb_ref, hb_ref, c_ref.at[1], n_ref.at[1], m_ref.at[1], True)


def _mlstm(qm, kmt, vm, gt, batch, seq):
    L = ML_CHUNK
    nc = seq // L
    t_tokens = batch * seq
    fwd = lambda b, c: b * nc + c
    bwd = lambda b, c: b * nc + (nc - 1 - c)
    row = lambda f: pl.BlockSpec((L, ML_WIDTH), lambda b, c: (f(b, c), 0))
    col = lambda f, r: pl.BlockSpec((r, L), lambda b, c: (0, f(b, c)))
    return pl.pallas_call(
        _mlstm_kernel,
        grid=(batch, nc),
        in_specs=[row(fwd), col(fwd, ML_WIDTH), row(fwd), col(fwd, N_GATES),
                  row(bwd), col(bwd, ML_WIDTH), row(bwd), col(bwd, N_GATES)],
        out_specs=[row(fwd), row(bwd)],
        out_shape=[jax.ShapeDtypeStruct((t_tokens, ML_WIDTH), F32)] * 2,
        scratch_shapes=[pltpu.VMEM((2, ML_WIDTH, L), F32), pltpu.VMEM((2, ML_WIDTH, L), F32),
                        pltpu.VMEM((2, 8, L), F32)],
        compiler_params=pltpu.CompilerParams(dimension_semantics=("parallel", "arbitrary")),
        name="mlstm",
    )(qm, kmt, vm, gt, qm, kmt, vm, gt)


def _out_ffn_kernel(x_ref, yat_ref, yp_ref, hf_ref, hb_ref, om_ref, ng_ref, gavg_ref,
                    woa_ref, wop_ref, wom_ref, l1g_ref, l1b_ref, wg_ref, wu_ref, wd_ref,
                    l2g_ref, l2b_ref, o_ref):
    hi = lax.Precision.HIGHEST
    x = x_ref[...]
    h = hf_ref[...] + hb_ref[...]
    mu = jnp.dot(h, gavg_ref[...], precision=hi, preferred_element_type=F32)
    hc = h - mu
    var = jnp.dot(hc * hc, gavg_ref[...], precision=hi, preferred_element_type=F32)
    hn = hc * lax.rsqrt(var + LN_EPS) * ng_ref[...]
    y_ml = (hn / (1.0 + jnp.exp(-om_ref[...]))).astype(BF16)
    y_at = yat_ref[...].T
    mix = (jnp.dot(y_at, woa_ref[...], preferred_element_type=F32)
           + jnp.dot(yp_ref[...], wop_ref[...], preferred_element_type=F32)
           + jnp.dot(y_ml, wom_ref[...], preferred_element_type=F32))
    x1 = _layer_norm(ALPHA * x + mix, l1g_ref[...], l1b_ref[...])
    x1b = x1.astype(BF16)
    gate = jnp.dot(x1b, wg_ref[...], preferred_element_type=F32)
    up = jnp.dot(x1b, wu_ref[...], preferred_element_type=F32)
    hid = (gate / (1.0 + jnp.exp(-gate)) * up).astype(BF16)
    ffn = jnp.dot(hid, wd_ref[...], preferred_element_type=F32)
    o_ref[...] = _layer_norm(ALPHA * x1 + ffn, l2g_ref[...], l2b_ref[...])


def _out_ffn(x, yat, yp, hf, hb, om, lw, gavg, *, tm=256):
    t_tokens = x.shape[0]
    row = lambda w: pl.BlockSpec((tm, w), lambda i: (i, 0))
    single = lambda a: pl.BlockSpec(a.shape, lambda i: (0,) * a.ndim, pipeline_mode=pl.Buffered(1))
    consts = [lw["norm_g"], gavg, lw["wo_a"], lw["wo_p"], lw["wo_m"], lw["ln1_g"], lw["ln1_b"],
              lw["w_gate"], lw["w_up"], lw["w_down"], lw["ln2_g"], lw["ln2_b"]]
    return pl.pallas_call(
        _out_ffn_kernel,
        grid=(t_tokens // tm,),
        in_specs=[row(D_MODEL), pl.BlockSpec((N_HEADS * D_V, tm), lambda i: (0, i)), row(POOL_WIDTH),
                  row(ML_WIDTH), row(ML_WIDTH), row(ML_WIDTH)] + [single(c) for c in consts],
        out_specs=row(D_MODEL),
        out_shape=jax.ShapeDtypeStruct((t_tokens, D_MODEL), F32),
        compiler_params=pltpu.CompilerParams(dimension_semantics=("parallel",),
                                             vmem_limit_bytes=VMEM_LIMIT),
        name="out_ffn",
    )(x, yat, yp, hf, hb, om, *consts)


def _head_pad_cols(w, n_heads, width, dst_lo):
    k = w.shape[0]
    w3 = w.reshape(k, n_heads, width)
    out = jnp.zeros((k, n_heads, D_HEAD_PAD), w.dtype)
    out = out.at[:, :, dst_lo:dst_lo + width].set(w3)
    return out.reshape(k, n_heads * D_HEAD_PAD)


def _rope_partner(w_rope):
    half = D_ROPE // 2
    return jnp.concatenate([-w_rope[..., half:], w_rope[..., :half]], axis=-1)


def _prep_layer_weights(l, w_in, q_norm_g, w_uq, kv_norm_g, w_ukv, w_pool, pool_scale, gate_bias,
                        norm_g, w_out, ln1_g, ln1_b, w_gate, w_up, w_down, ln2_g, ln2_b):
    wi = w_in[l]
    c = np.cumsum((0, 256, 128, 32, 256, 256, 256, 256, 256, 16))
    w_cq, w_ckv, w_kr, w_pl, w_qm, w_km, w_vm, w_om, w_g = (wi[:, c[j]:c[j + 1]] for j in range(9))
    kr_pad = jnp.zeros((D_MODEL, D_HEAD_PAD), F32).at[:, D_NOPE:D_NOPE + D_ROPE].set(w_kr)
    krs_pad = jnp.zeros((D_MODEL, D_HEAD_PAD), F32).at[:, D_NOPE:D_NOPE + D_ROPE].set(_rope_partner(w_kr))
    wa = jnp.concatenate([w_cq, w_ckv, kr_pad, krs_pad, w_pl, w_qm, w_vm, w_om], axis=1).astype(BF16)
    wt = jnp.concatenate([w_km.T * ML_DIM ** -0.5, w_g.T], axis=0).astype(BF16)
    uq = w_uq[l].reshape(Q_RANK, N_HEADS, D_NOPE + D_ROPE)
    uq_plain = jnp.zeros((Q_RANK, N_HEADS, D_HEAD_PAD), F32).at[:, :, :D_NOPE + D_ROPE].set(uq)
    uq_part = jnp.zeros((Q_RANK, N_HEADS, D_HEAD_PAD), F32).at[:, :, D_NOPE:D_NOPE + D_ROPE].set(
        _rope_partner(uq[:, :, D_NOPE:]))
    wq2 = jnp.concatenate([uq_plain.reshape(Q_RANK, -1), uq_part.reshape(Q_RANK, -1)], axis=1).astype(BF16)
    ukv = w_ukv[l].reshape(KV_RANK, N_HEADS, D_NOPE + D_V)
    wuk = _head_pad_cols(ukv[:, :, :D_NOPE].reshape(KV_RANK, -1), N_HEADS, D_NOPE, 0).astype(BF16)
    wuvt = ukv[:, :, D_NOPE:].reshape(KV_RANK, -1).T.astype(BF16)
    w_bd = jnp.zeros((POOL_WIDTH, POOL_WIDTH), F32)
    for g in range(len(POOL_WINDOWS)):
        s = slice(g * POOL_GROUP, (g + 1) * POOL_GROUP)
        w_bd = w_bd.at[s, s].set(w_pool[l, g])
    wo = w_out[l].astype(BF16)
    a_end = N_HEADS * D_V
    return {
        "wa": wa, "wt": wt, "gate_bias": gate_bias[l].reshape(N_GATES, 1),
        "q_norm_g": q_norm_g[l].reshape(1, -1), "wq2": wq2, "kv_norm_g": kv_norm_g[l].reshape(1, -1),
        "wuk": wuk, "wuvt": wuvt, "w_pool": w_bd.astype(BF16), "pool_scale": pool_scale[l].reshape(1, -1),
        "norm_g": norm_g[l].reshape(1, -1),
        "wo_a": wo[:a_end], "wo_p": wo[a_end:a_end + POOL_WIDTH], "wo_m": wo[a_end + POOL_WIDTH:],
        "ln1_g": ln1_g[l].reshape(1, -1), "ln1_b": ln1_b[l].reshape(1, -1),
        "w_gate": w_gate[l].astype(BF16), "w_up": w_up[l].astype(BF16), "w_down": w_down[l].astype(BF16),
        "ln2_g": ln2_g[l].reshape(1, -1), "ln2_b": ln2_b[l].reshape(1, -1),
    }


def _rope_tables(seq):
    half = D_ROPE // 2
    inv = 1.0 / (ROPE_THETA ** (jnp.arange(0, D_ROPE, 2, dtype=F32) / D_ROPE))
    ang = jnp.arange(seq, dtype=F32)[:, None] * inv[None, :]
    cos, sin = jnp.cos(ang), jnp.sin(ang)
    one = jnp.ones((seq, D_NOPE), F32)
    zero_n = jnp.zeros((seq, D_NOPE), F32)
    zero_p = jnp.zeros((seq, D_HEAD_PAD - D_NOPE - D_ROPE), F32)
    q_scale = (D_NOPE + D_ROPE) ** -0.5 * LOG2E
    cq = jnp.concatenate([one, cos, cos, zero_p], axis=1) * q_scale
    sq = jnp.concatenate([zero_n, sin, sin, zero_p], axis=1) * q_scale
    ck = jnp.concatenate([zero_n, cos, cos, zero_p], axis=1)
    sk = jnp.concatenate([zero_n, sin, sin, zero_p], axis=1)
    return cq, sq, ck, sk


def _group_average_matrix():
    g = np.arange(ML_WIDTH) // ML_DIM
    return jnp.asarray((g[:, None] == g[None, :]).astype(np.float32) / ML_DIM)


def _trunk(x, ln_in, layers, gavg):
    batch, seq, _ = x.shape
    tabs = _rope_tables(seq)
    xf = x.reshape(batch * seq, D_MODEL)
    for l, lw in enumerate(layers):
        outs = _in_proj(xf, lw, tabs, seq, ln=ln_in if l == 0 else None)
        if l == 0:
            xf, outs = outs[0], outs[1:]
        q, k, vt, xp, qm, vm, om, kmt, gt = outs
        yat = _attention(q, k, vt, batch, seq)
        yp = _pool(xp, lw["w_pool"], lw["pool_scale"], batch, seq)
        hf, hb = _mlstm(qm, kmt, vm, gt, batch, seq)
        xf = _out_ffn(xf, yat, yp, hf, hb, om, lw, gavg)
    return xf.reshape(batch, seq, D_MODEL)


def kernel(x_prompt, x_sample, ln_in_g, ln_in_b, w_in, q_norm_g, w_uq, kv_norm_g, w_ukv, w_pool, pool_scale,
           mlstm_gate_bias, mlstm_norm_g, w_out, ln1_g, ln1_b, w_gate, w_up, w_down, ln2_g, ln2_b):
    layers = [_prep_layer_weights(l, w_in, q_norm_g, w_uq, kv_norm_g, w_ukv, w_pool, pool_scale,
                                  mlstm_gate_bias, mlstm_norm_g, w_out, ln1_g, ln1_b, w_gate, w_up, w_down,
                                  ln2_g, ln2_b) for l in range(DEPTH)]
    ln_in = (ln_in_g.reshape(1, -1), ln_in_b.reshape(1, -1))
    gavg = _group_average_matrix()
    return (_trunk(x_prompt, ln_in, layers, gavg), _trunk(x_sample, ln_in, layers, gavg))
```

```python
import functools
import math

import jax
import jax.numpy as jnp
import numpy as np
from jax import lax
from jax.experimental import pallas as pl
from jax.experimental.pallas import tpu as pltpu

F32 = jnp.float32
BF16 = jnp.bfloat16

D_MODEL = 1024
DEPTH = 4
N_HEADS = 8
D_NOPE = 64
D_ROPE = 32
D_V = 64
Q_RANK = 256
KV_RANK = 128
ROPE_THETA = 10000.0
D_HEAD_PAD = 128
POOL_WIDTH = 256
POOL_WINDOWS = (2, 4, 8, 16)
POOL_GROUP = 64
POOL_HALO = 8
ML_HEADS = 4
ML_DIM = 64
ML_WIDTH = 256
ML_CHUNK = 128
N_GATES = 16
NEG_BIG = -1e30
D_FF = 2816
ALPHA = (2 * DEPTH) ** 0.25
LN_EPS = 1e-5
LOG2E = 1.4426950408889634

VMEM_LIMIT = 56 * 1024 * 1024

_PA_CQ, _PA_CKV, _PA_KR, _PA_KRS, _PA_POOL, _PA_QM, _PA_VM, _PA_OM, _PA_END = (
    0, 256, 384, 512, 640, 896, 1152, 1408, 1664)

_NT = (((1,), (1,)), ((), ()))


def _layer_norm(x, g, b):
    mu = jnp.mean(x, axis=-1, keepdims=True)
    xc = x - mu
    var = jnp.mean(xc * xc, axis=-1, keepdims=True)
    return xc * lax.rsqrt(var + LN_EPS) * g + b


def _rms_norm(x, g):
    return x * lax.rsqrt(jnp.mean(x * x, axis=-1, keepdims=True) + LN_EPS) * g


def _in_proj_kernel(*refs, apply_ln):
    if apply_ln:
        (x_ref, lng_ref, lnb_ref, wa_ref, wt_ref, gb_ref, qg_ref, wq_ref, kvg_ref, wuk_ref, wuvt_ref,
         cq_ref, sq_ref, ck_ref, sk_ref,
         xn_ref, q_ref, k_ref, vt_ref, pool_ref, qm_ref, vm_ref, om_ref, kmt_ref, gt_ref) = refs
    else:
        (x_ref, wa_ref, wt_ref, gb_ref, qg_ref, wq_ref, kvg_ref, wuk_ref, wuvt_ref,
         cq_ref, sq_ref, ck_ref, sk_ref,
         q_ref, k_ref, vt_ref, pool_ref, qm_ref, vm_ref, om_ref, kmt_ref, gt_ref) = refs
    x = x_ref[...]
    if apply_ln:
        x = _layer_norm(x, lng_ref[...], lnb_ref[...])
        xn_ref[...] = x
    xb = x.astype(BF16)
    pa = jnp.dot(xb, wa_ref[...], preferred_element_type=F32)
    pt = lax.dot_general(wt_ref[...], xb, _NT, preferred_element_type=F32)
    kmt_ref[...] = pt[:ML_WIDTH].astype(BF16)
    gt_ref[...] = pt[ML_WIDTH:] + gb_ref[...]

    pool_ref[...] = pa[:, _PA_POOL:_PA_QM]
    qm_ref[...] = pa[:, _PA_QM:_PA_VM].astype(BF16)
    vm_ref[...] = pa[:, _PA_VM:_PA_OM].astype(BF16)
    om_ref[...] = pa[:, _PA_OM:_PA_END]

    cqn = _rms_norm(pa[:, _PA_CQ:_PA_CKV], qg_ref[...]).astype(BF16)
    q2t = lax.dot_general(wq_ref[...], cqn, _NT, preferred_element_type=F32)
    cq, sq = cq_ref[...], sq_ref[...]
    hw = N_HEADS * D_HEAD_PAD
    for h in range(N_HEADS):
        lo = h * D_HEAD_PAD
        qh = q2t[lo:lo + D_HEAD_PAD] * cq + q2t[hw + lo:hw + lo + D_HEAD_PAD] * sq
        q_ref[lo:lo + D_HEAD_PAD, :] = qh.astype(BF16)

    ckvn = _rms_norm(pa[:, _PA_CKV:_PA_KR], kvg_ref[...]).astype(BF16)
    kn = jnp.dot(ckvn, wuk_ref[...], preferred_element_type=F32)
    kr = pa[:, _PA_KR:_PA_KRS] * ck_ref[...] + pa[:, _PA_KRS:_PA_POOL] * sk_ref[...]
    for h in range(N_HEADS):
        lo = h * D_HEAD_PAD
        k_ref[:, lo:lo + D_HEAD_PAD] = (kn[:, lo:lo + D_HEAD_PAD] + kr).astype(BF16)
    vt = lax.dot_general(wuvt_ref[...], ckvn, _NT, preferred_element_type=F32)
    vt_ref[...] = vt.astype(BF16)


def _const_spec(shape):
    nd = len(shape)
    return pl.BlockSpec(shape, lambda *_: (0,) * nd)


def _in_proj(x, lw, tabs, seq, *, ln=None, tm=256):
    t_tokens = x.shape[0]
    n_seq_tiles = seq // tm
    apply_ln = ln is not None
    row = lambda w: pl.BlockSpec((tm, w), lambda i: (i, 0))
    colt = lambda r: pl.BlockSpec((r, tm), lambda i: (0, i))
    tab = pl.BlockSpec((tm, D_HEAD_PAD), lambda i: (i % n_seq_tiles, 0))
    consts = [lw["wa"], lw["wt"], lw["gate_bias"], lw["q_norm_g"], lw["wq2"], lw["kv_norm_g"],
              lw["wuk"], lw["wuvt"]]
    in_specs = [row(D_MODEL)]
    args = [x]
    if apply_ln:
        in_specs += [_const_spec((1, D_MODEL))] * 2
        args += [ln[0], ln[1]]
    tab_t = pl.BlockSpec((D_HEAD_PAD, tm), lambda i: (0, i % n_seq_tiles))
    in_specs += [_const_spec(c.shape) for c in consts] + [tab_t, tab_t, tab, tab]
    args += consts + list(tabs)
    hw = N_HEADS * D_HEAD_PAD
    out_shape, out_specs = [], []
    if apply_ln:
        out_shape.append(jax.ShapeDtypeStruct((t_tokens, D_MODEL), F32))
        out_specs.append(row(D_MODEL))
    out_shape += [
        jax.ShapeDtypeStruct((hw, t_tokens), BF16),
        jax.ShapeDtypeStruct((t_tokens, hw), BF16),
        jax.ShapeDtypeStruct((N_HEADS * D_V, t_tokens), BF16),
        jax.ShapeDtypeStruct((t_tokens, POOL_WIDTH), F32),
        jax.ShapeDtypeStruct((t_tokens, ML_WIDTH), BF16),
        jax.ShapeDtypeStruct((t_tokens, ML_WIDTH), BF16),
        jax.ShapeDtypeStruct((t_tokens, ML_WIDTH), F32),
        jax.ShapeDtypeStruct((ML_WIDTH, t_tokens), BF16),
        jax.ShapeDtypeStruct((N_GATES, t_tokens), F32),
    ]
    out_specs += [colt(hw), row(hw), colt(N_HEADS * D_V), row(POOL_WIDTH), row(ML_WIDTH), row(ML_WIDTH),
                  row(ML_WIDTH), colt(ML_WIDTH), colt(N_GATES)]
    return pl.pallas_call(
        functools.partial(_in_proj_kernel, apply_ln=apply_ln),
        grid=(t_tokens // tm,),
        in_specs=in_specs,
        out_specs=out_specs,
        out_shape=out_shape,
        compiler_params=pltpu.CompilerParams(dimension_semantics=("parallel",),
                                             vmem_limit_bytes=VMEM_LIMIT),
        name="in_proj_ln" if apply_ln else "in_proj",
    )(*args)


def _attention_kernel(q_ref, k_ref, vt_ref, o_ref, s_ref, *, tk, sub):
    tq = q_ref.shape[1]
    n_sub = tq // sub
    n_k = k_ref.shape[0] // tk
    assert n_k % 2 == 0

    def scores(t, slot):
        kblk = k_ref[pl.ds(pl.multiple_of(t * tk, tk), tk), :]
        for c in range(n_sub):
            s_ref[slot, c] = jnp.dot(kblk, q_ref[:, c * sub:(c + 1) * sub], preferred_element_type=F32)

    def accumulate(t, slot, carry):
        vblk = vt_ref[:, pl.ds(pl.multiple_of(t * tk, tk), tk)]
        out = []
        for c in range(n_sub):
            m, l, acc = carry[c]
            s = s_ref[slot, c]
            m_new = jnp.maximum(m, jnp.max(s, axis=0, keepdims=True))
            alpha = jnp.exp2(m - m_new)
            p = jnp.exp2(s - m_new)
            l = alpha * l + jnp.sum(p, axis=0, keepdims=True)
            pv = jnp.dot(vblk, p.astype(BF16), preferred_element_type=F32)
            out.append((m_new, l, alpha * acc + pv))
        return tuple(out)

    def body(j, carry):
        t = 2 * j
        scores(t + 1, 1)
        carry = accumulate(t, 0, carry)
        scores(t + 2, 0)
        return accumulate(t + 1, 1, carry)

    init = tuple((jnp.full((1, sub), NEG_BIG, F32), jnp.zeros((1, sub), F32), jnp.zeros((D_V, sub), F32))
                 for _ in range(n_sub))
    scores(0, 0)
    carry = lax.fori_loop(0, n_k // 2 - 1, body, init)
    scores(n_k - 1, 1)
    carry = accumulate(n_k - 2, 0, carry)
    carry = accumulate(n_k - 1, 1, carry)
    for c in range(n_sub):
        _, l, acc = carry[c]
        o_ref[:, c * sub:(c + 1) * sub] = (acc / l).astype(BF16)


def _attention(q, k, vt, batch, seq, *, tq=1024, tk=512, sub=256):
    n_q = seq // tq
    t_tokens = batch * seq
    k3 = k.reshape(batch, seq, N_HEADS * D_HEAD_PAD)
    return pl.pallas_call(
        functools.partial(_attention_kernel, tk=tk, sub=sub),
        grid=(batch, N_HEADS, n_q),
        in_specs=[
            pl.BlockSpec((D_HEAD_PAD, tq), lambda b, h, i: (h, b * n_q + i)),
            pl.BlockSpec((None, seq, D_HEAD_PAD), lambda b, h, i: (b, 0, h)),
            pl.BlockSpec((D_V, seq), lambda b, h, i: (h, b)),
        ],
        out_specs=pl.BlockSpec((D_V, tq), lambda b, h, i: (h, b * n_q + i)),
        out_shape=jax.ShapeDtypeStruct((N_HEADS * D_V, t_tokens), BF16),
        scratch_shapes=[pltpu.VMEM((2, tq // sub, tk, sub), F32)],
        compiler_params=pltpu.CompilerParams(
            dimension_semantics=("parallel", "parallel", "arbitrary"), vmem_limit_bytes=VMEM_LIMIT),
        name="attention",
    )(q, k3, vt)


def _pool_kernel(x_ref, prev_ref, next_ref, w_ref, scale_ref, o_ref, *, seq):
    i = pl.program_id(1)
    ts = x_ref.shape[0]
    x = x_ref[...]
    prev = jnp.where(i == 0, 0.0, prev_ref[...])
    nxt = jnp.where(i == pl.num_programs(1) - 1, 0.0, next_ref[...])
    xe = jnp.concatenate([prev, x, nxt], axis=0)
    n = ts + 2 * POOL_HALO
    p2 = xe[0:n - 1] + xe[1:n]
    p4 = p2[0:n - 3] + p2[2:n - 1]
    p8 = p4[0:n - 7] + p4[4:n - 3]
    p16 = p8[0:n - 15] + p8[8:n - 7]
    sums = {2: p2, 4: p4, 8: p8, 16: p16}
    t = i * ts + lax.broadcasted_iota(jnp.int32, (ts, 1), 0)
    lane = lax.broadcasted_iota(jnp.int32, (1, POOL_WIDTH), 1)
    y = jnp.zeros((ts, POOL_WIDTH), F32)
    for g, w in enumerate(POOL_WINDOWS):
        start = POOL_HALO - w // 2
        win = sums[w][start:start + ts]
        cnt = (jnp.minimum(t + w // 2, seq) - jnp.maximum(t - w // 2, 0)).astype(F32)
        y = jnp.where(lane // POOL_GROUP == g, win / cnt - x, y)
    out = jnp.dot(y.astype(BF16), w_ref[...], preferred_element_type=F32) * scale_ref[...]
    o_ref[...] = out.astype(BF16)


def _pool(xp, w_bd, scale, batch, seq, *, ts=512):
    n_t = seq // ts
    hb = ts // POOL_HALO
    x3 = xp.reshape(batch, seq, POOL_WIDTH)
    out = pl.pallas_call(
        functools.partial(_pool_kernel, seq=seq),
        grid=(batch, n_t),
        in_specs=[
            pl.BlockSpec((None, ts, POOL_WIDTH), lambda b, i: (b, i, 0)),
            pl.BlockSpec((None, POOL_HALO, POOL_WIDTH), lambda b, i: (b, jnp.maximum(i * hb - 1, 0), 0)),
            pl.BlockSpec((None, POOL_HALO, POOL_WIDTH),
                         lambda b, i: (b, jnp.minimum((i + 1) * hb, seq // POOL_HALO - 1), 0)),
            _const_spec((POOL_WIDTH, POOL_WIDTH)),
            _const_spec((1, POOL_WIDTH)),
        ],
        out_specs=pl.BlockSpec((None, ts, POOL_WIDTH), lambda b, i: (b, i, 0)),
        out_shape=jax.ShapeDtypeStruct((batch, seq, POOL_WIDTH), BF16),
        compiler_params=pltpu.CompilerParams(dimension_semantics=("parallel", "parallel")),
        name="pool",
    )(x3, x3, x3, w_bd, scale)
    return out.reshape(batch * seq, POOL_WIDTH)


def _log_sigmoid(x):
    return -(jnp.maximum(-x, 0.0) + jnp.log1p(jnp.exp(-jnp.abs(x))))


def _mlstm_direction(q_ref, kt_ref, v_ref, g_ref, h_ref, c_ref, n_ref, m_ref, reverse):
    L = ML_CHUNK
    hi = lax.Precision.HIGHEST
    r = lax.broadcasted_iota(jnp.int32, (L, L), 0)
    c = lax.broadcasted_iota(jnp.int32, (L, L), 1)
    if reverse:
        inc = (c >= r)
    else:
        inc = (c <= r)
    inc_f = inc.astype(F32)
    g = g_ref[...]
    i_base = 8 if reverse else 0
    f_base = i_base + ML_HEADS
    logf = _log_sigmoid(g)
    b_rows = lax.dot_general(logf, inc_f, _NT, precision=hi, preferred_element_type=F32)
    q = q_ref[...]
    kt = kt_ref[...]
    v = v_ref[...]
    ones_b = jnp.ones((L, L), BF16)
    lane_q = lax.broadcasted_iota(jnp.int32, (L, ML_WIDTH), 1)
    lane = lax.broadcasted_iota(jnp.int32, (L, L), 1)
    last = 0 if reverse else L - 1
    c_bf = c_ref[...].astype(BF16)
    n_bf = n_ref[...].astype(BF16)
    outs = []
    for h in range(ML_HEADS):
        b_row = b_rows[f_base + h:f_base + h + 1]
        i_row = g[i_base + h:i_base + h + 1]
        b_col = lax.dot_general(inc_f, jnp.broadcast_to(logf[f_base + h:f_base + h + 1], (L, L)), _NT,
                                precision=hi, preferred_element_type=F32)
        m_old = m_ref[h:h + 1, :]
        dm = jnp.where(inc, (b_col - b_row) + i_row, NEG_BIG)
        m_inter = b_col + m_old
        m_j = jnp.maximum(m_inter, jnp.max(dm, axis=1, keepdims=True))
        w_intra = jnp.exp(dm - m_j)
        w_inter = jnp.exp(m_inter - m_j)
        qh = jnp.where(lane_q // ML_DIM == h, q, jnp.zeros_like(q))
        kth = kt[h * ML_DIM:(h + 1) * ML_DIM]
        scores = jnp.dot(qh, kt, preferred_element_type=F32) * w_intra
        sb = scores.astype(BF16)
        pair = (h // 2) * L
        v_pair = v[:, pair:pair + L]
        num = w_inter * jnp.dot(qh, c_bf, preferred_element_type=F32) \
            + jnp.dot(sb, v_pair, preferred_element_type=F32)
        den = w_inter * jnp.dot(qh, n_bf, preferred_element_type=F32) \
            + jnp.dot(sb, ones_b, preferred_element_type=F32)
        outs.append(num / jnp.maximum(jnp.abs(den), jnp.exp(-m_j)))
        b_last = b_col[last:last + 1]
        g_row = (b_last - b_row) + i_row
        m_new = jnp.maximum(b_last + m_old, jnp.max(g_row, axis=1, keepdims=True))
        decay = jnp.exp(b_last + m_old - m_new)
        w_s = jnp.exp(g_row - m_new)
        kw = (kth.astype(F32) * w_s).astype(BF16)
        rows = slice(h * ML_DIM, (h + 1) * ML_DIM)
        c_ref[rows, :] = decay * c_ref[rows, :] + jnp.dot(kw, v_pair, preferred_element_type=F32)
        n_ref[rows, :] = decay * n_ref[rows, :] + jnp.dot(kw, ones_b, preferred_element_type=F32)
        m_ref[h:h + 1, :] = m_new
    for p in range(ML_HEADS // 2):
        h_ref[:, p * L:(p + 1) * L] = jnp.where(lane < ML_DIM, outs[2 * p], outs[2 * p + 1])


def _mlstm_kernel(qf_ref, ktf_ref, vf_ref, gf_ref, qb_ref, ktb_ref, vb_ref, gb_ref,
                  hf_ref, hb_ref, c_ref, n_ref, m_ref):
    @pl.when(pl.program_id(1) == 0)
    def _():
        c_ref[...] = jnp.zeros_like(c_ref)
        n_ref[...] = jnp.zeros_like(n_ref)
        m_ref[...] = jnp.zeros_like(m_ref)

    _mlstm_direction(qf_ref, ktf_ref, vf_ref, gf_ref, hf_ref, c_ref.at[0], n_ref.at[0], m_ref.at[0], False)
    _mlstm_direction(qb_ref, ktb_ref, vb_ref, gb_ref, hb_ref, c_ref.at[1], n_ref.at[1], m_ref.at[1], True)


def _mlstm(qm, kmt, vm, gt, batch, seq):
    L = ML_CHUNK
    nc = seq // L
    t_tokens = batch * seq
    fwd = lambda b, c: b * nc + c
    bwd = lambda b, c: b * nc + (nc - 1 - c)
    row = lambda f: pl.BlockSpec((L, ML_WIDTH), lambda b, c: (f(b, c), 0))
    col = lambda f, r: pl.BlockSpec((r, L), lambda b, c: (0, f(b, c)))
    return pl.pallas_call(
        _mlstm_kernel,
        grid=(batch, nc),
        in_specs=[row(fwd), col(fwd, ML_WIDTH), row(fwd), col(fwd, N_GATES),
                  row(bwd), col(bwd, ML_WIDTH), row(bwd), col(bwd, N_GATES)],
        out_specs=[row(fwd), row(bwd)],
        out_shape=[jax.ShapeDtypeStruct((t_tokens, ML_WIDTH), F32)] * 2,
        scratch_shapes=[pltpu.VMEM((2, ML_WIDTH, L), F32), pltpu.VMEM((2, ML_WIDTH, L), F32),
                        pltpu.VMEM((2, 8, L), F32)],
        compiler_params=pltpu.CompilerParams(dimension_semantics=("parallel", "arbitrary")),
        name="mlstm",
    )(qm, kmt, vm, gt, qm, kmt, vm, gt)


def _out_ffn_kernel(x_ref, yat_ref, yp_ref, hf_ref, hb_ref, om_ref, ng_ref, gavg_ref,
                    woa_ref, wop_ref, wom_ref, l1g_ref, l1b_ref, wg_ref, wu_ref, wd_ref,
                    l2g_ref, l2b_ref, o_ref):
    hi = lax.Precision.HIGHEST
    x = x_ref[...]
    h = hf_ref[...] + hb_ref[...]
    mu = jnp.dot(h, gavg_ref[...], precision=hi, preferred_element_type=F32)
    hc = h - mu
    var = jnp.dot(hc * hc, gavg_ref[...], precision=hi, preferred_element_type=F32)
    hn = hc * lax.rsqrt(var + LN_EPS) * ng_ref[...]
    y_ml = (hn / (1.0 + jnp.exp(-om_ref[...]))).astype(BF16)
    y_at = yat_ref[...].T
    mix = (jnp.dot(y_at, woa_ref[...], preferred_element_type=F32)
           + jnp.dot(yp_ref[...], wop_ref[...], preferred_element_type=F32)
           + jnp.dot(y_ml, wom_ref[...], preferred_element_type=F32))
    x1 = _layer_norm(ALPHA * x + mix, l1g_ref[...], l1b_ref[...])
    x1b = x1.astype(BF16)
    gate = jnp.dot(x1b, wg_ref[...], preferred_element_type=F32)
    up = jnp.dot(x1b, wu_ref[...], preferred_element_type=F32)
    hid = (gate / (1.0 + jnp.exp(-gate)) * up).astype(BF16)
    ffn = jnp.dot(hid, wd_ref[...], preferred_element_type=F32)
    o_ref[...] = _layer_norm(ALPHA * x1 + ffn, l2g_ref[...], l2b_ref[...])


def _out_ffn(x, yat, yp, hf, hb, om, lw, gavg, *, tm=256):
    t_tokens = x.shape[0]
    row = lambda w: pl.BlockSpec((tm, w), lambda i: (i, 0))
    single = lambda a: pl.BlockSpec(a.shape, lambda i: (0,) * a.ndim, pipeline_mode=pl.Buffered(1))
    consts = [lw["norm_g"], gavg, lw["wo_a"], lw["wo_p"], lw["wo_m"], lw["ln1_g"], lw["ln1_b"],
              lw["w_gate"], lw["w_up"], lw["w_down"], lw["ln2_g"], lw["ln2_b"]]
    return pl.pallas_call(
        _out_ffn_kernel,
        grid=(t_tokens // tm,),
        in_specs=[row(D_MODEL), pl.BlockSpec((N_HEADS * D_V, tm), lambda i: (0, i)), row(POOL_WIDTH),
                  row(ML_WIDTH), row(ML_WIDTH), row(ML_WIDTH)] + [single(c) for c in consts],
        out_specs=row(D_MODEL),
        out_shape=jax.ShapeDtypeStruct((t_tokens, D_MODEL), F32),
        compiler_params=pltpu.CompilerParams(dimension_semantics=("parallel",),
                                             vmem_limit_bytes=VMEM_LIMIT),
        name="out_ffn",
    )(x, yat, yp, hf, hb, om, *consts)


def _head_pad_cols(w, n_heads, width, dst_lo):
    k = w.shape[0]
    w3 = w.reshape(k, n_heads, width)
    out = jnp.zeros((k, n_heads, D_HEAD_PAD), w.dtype)
    out = out.at[:, :, dst_lo:dst_lo + width].set(w3)
    return out.reshape(k, n_heads * D_HEAD_PAD)


def _rope_partner(w_rope):
    half = D_ROPE // 2
    return jnp.concatenate([-w_rope[..., half:], w_rope[..., :half]], axis=-1)


def _prep_layer_weights(l, w_in, q_norm_g, w_uq, kv_norm_g, w_ukv, w_pool, pool_scale, gate_bias,
                        norm_g, w_out, ln1_g, ln1_b, w_gate, w_up, w_down, ln2_g, ln2_b):
    wi = w_in[l]
    c = np.cumsum((0, 256, 128, 32, 256, 256, 256, 256, 256, 16))
    w_cq, w_ckv, w_kr, w_pl, w_qm, w_km, w_vm, w_om, w_g = (wi[:, c[j]:c[j + 1]] for j in range(9))
    kr_pad = jnp.zeros((D_MODEL, D_HEAD_PAD), F32).at[:, D_NOPE:D_NOPE + D_ROPE].set(w_kr)
    krs_pad = jnp.zeros((D_MODEL, D_HEAD_PAD), F32).at[:, D_NOPE:D_NOPE + D_ROPE].set(_rope_partner(w_kr))
    wa = jnp.concatenate([w_cq, w_ckv, kr_pad, krs_pad, w_pl, w_qm, w_vm, w_om], axis=1).astype(BF16)
    wt = jnp.concatenate([w_km.T * ML_DIM ** -0.5, w_g.T], axis=0).astype(BF16)
    uq = w_uq[l].reshape(Q_RANK, N_HEADS, D_NOPE + D_ROPE)
    uq_plain = jnp.zeros((Q_RANK, N_HEADS, D_HEAD_PAD), F32).at[:, :, :D_NOPE + D_ROPE].set(uq)
    uq_part = jnp.zeros((Q_RANK, N_HEADS, D_HEAD_PAD), F32).at[:, :, D_NOPE:D_NOPE + D_ROPE].set(
        _rope_partner(uq[:, :, D_NOPE:]))
    wq2 = jnp.concatenate([uq_plain.reshape(Q_RANK, -1), uq_part.reshape(Q_RANK, -1)], axis=1).T.astype(BF16)
    ukv = w_ukv[l].reshape(KV_RANK, N_HEADS, D_NOPE + D_V)
    wuk = _head_pad_cols(ukv[:, :, :D_NOPE].reshape(KV_RANK, -1), N_HEADS, D_NOPE, 0).astype(BF16)
    wuvt = ukv[:, :, D_NOPE:].reshape(KV_RANK, -1).T.astype(BF16)
    w_bd = jnp.zeros((POOL_WIDTH, POOL_WIDTH), F32)
    for g in range(len(POOL_WINDOWS)):
        s = slice(g * POOL_GROUP, (g + 1) * POOL_GROUP)
        w_bd = w_bd.at[s, s].set(w_pool[l, g])
    wo = w_out[l].astype(BF16)
    a_end = N_HEADS * D_V
    return {
        "wa": wa, "wt": wt, "gate_bias": gate_bias[l].reshape(N_GATES, 1),
        "q_norm_g": q_norm_g[l].reshape(1, -1), "wq2": wq2, "kv_norm_g": kv_norm_g[l].reshape(1, -1),
        "wuk": wuk, "wuvt": wuvt, "w_pool": w_bd.astype(BF16), "pool_scale": pool_scale[l].reshape(1, -1),
        "norm_g": norm_g[l].reshape(1, -1),
        "wo_a": wo[:a_end], "wo_p": wo[a_end:a_end + POOL_WIDTH], "wo_m": wo[a_end + POOL_WIDTH:],
        "ln1_g": ln1_g[l].reshape(1, -1), "ln1_b": ln1_b[l].reshape(1, -1),
        "w_gate": w_gate[l].astype(BF16), "w_up": w_up[l].astype(BF16), "w_down": w_down[l].astype(BF16),
        "ln2_g": ln2_g[l].reshape(1, -1), "ln2_b": ln2_b[l].reshape(1, -1),
    }


def _rope_tables(seq):
    half = D_ROPE // 2
    inv = 1.0 / (ROPE_THETA ** (jnp.arange(0, D_ROPE, 2, dtype=F32) / D_ROPE))
    ang = jnp.arange(seq, dtype=F32)[:, None] * inv[None, :]
    cos, sin = jnp.cos(ang), jnp.sin(ang)
    one = jnp.ones((seq, D_NOPE), F32)
    zero_n = jnp.zeros((seq, D_NOPE), F32)
    zero_p = jnp.zeros((seq, D_HEAD_PAD - D_NOPE - D_ROPE), F32)
    q_scale = (D_NOPE + D_ROPE) ** -0.5 * LOG2E
    cq = jnp.concatenate([one, cos, cos, zero_p], axis=1) * q_scale
    sq = jnp.concatenate([zero_n, sin, sin, zero_p], axis=1) * q_scale
    ck = jnp.concatenate([zero_n, cos, cos, zero_p], axis=1)
    sk = jnp.concatenate([zero_n, sin, sin, zero_p], axis=1)
    return cq.T, sq.T, ck, sk


def _group_average_matrix():
    g = np.arange(ML_WIDTH) // ML_DIM
    return jnp.asarray((g[:, None] == g[None, :]).astype(np.float32) / ML_DIM)


def _trunk(x, ln_in, layers, gavg):
    batch, seq, _ = x.shape
    tabs = _rope_tables(seq)
    xf = x.reshape(batch * seq, D_MODEL)
    for l, lw in enumerate(layers):
        outs = _in_proj(xf, lw, tabs, seq, ln=ln_in if l == 0 else None)
        if l == 0:
            xf, outs = outs[0], outs[1:]
        q, k, vt, xp, qm, vm, om, kmt, gt = outs
        yat = _attention(q, k, vt, batch, seq)
        yp = _pool(xp, lw["w_pool"], lw["pool_scale"], batch, seq)
        hf, hb = _mlstm(qm, kmt, vm, gt, batch, seq)
        xf = _out_ffn(xf, yat, yp, hf, hb, om, lw, gavg)
    return xf.reshape(batch, seq, D_MODEL)


def kernel(x_prompt, x_sample, ln_in_g, ln_in_b, w_in, q_norm_g, w_uq, kv_norm_g, w_ukv, w_pool, pool_scale,
           mlstm_gate_bias, mlstm_norm_g, w_out, ln1_g, ln1_b, w_gate, w_up, w_down, ln2_g, ln2_b):
    layers = [_prep_layer_weights(l, w_in, q_norm_g, w_uq, kv_norm_g, w_ukv, w_pool, pool_scale,
                                  mlstm_gate_bias, mlstm_norm_g, w_out, ln1_g, ln1_b, w_gate, w_up, w_down,
                                  ln2_g, ln2_b) for l in range(DEPTH)]
    ln_in = (ln_in_g.reshape(1, -1), ln_in_b.reshape(1, -1))
    gavg = _group_average_matrix()
    return (_trunk(x_prompt, ln_in, layers, gavg), _trunk(x_sample, ln_in, layers, gavg))
```

```python
import functools
import math

import jax
import jax.numpy as jnp
import numpy as np
from jax import lax
from jax.experimental import pallas as pl
from jax.experimental.pallas import tpu as pltpu

F32 = jnp.float32
BF16 = jnp.bfloat16

D_MODEL = 1024
DEPTH = 4
N_HEADS = 8
D_NOPE = 64
D_ROPE = 32
D_V = 64
BF16_ROWS = 16
ACC_ROWS = 72
Q_RANK = 256
KV_RANK = 128
ROPE_THETA = 10000.0
D_HEAD_PAD = 128
POOL_WIDTH = 256
POOL_WINDOWS = (2, 4, 8, 16)
POOL_GROUP = 64
POOL_HALO = 8
ML_HEADS = 4
ML_DIM = 64
ML_WIDTH = 256
ML_CHUNK = 128
ML_VT_HEAD = 80
ML_VT_ROWS = ML_HEADS * ML_VT_HEAD
N_GATES = 16
NEG_BIG = -1e30
D_FF = 2816
ALPHA = (2 * DEPTH) ** 0.25
LN_EPS = 1e-5
LOG2E = 1.4426950408889634

VMEM_LIMIT = 56 * 1024 * 1024

_PA_CQ, _PA_CKV, _PA_KR, _PA_KRS, _PA_POOL, _PA_KM, _PA_OM, _PA_GC, _PA_END = (
    0, 256, 384, 512, 640, 896, 1152, 1408, 1536)

_NT = (((1,), (1,)), ((), ()))


def _layer_norm(x, g, b):
    mu = jnp.mean(x, axis=-1, keepdims=True)
    xc = x - mu
    var = jnp.mean(xc * xc, axis=-1, keepdims=True)
    return xc * lax.rsqrt(var + LN_EPS) * g + b


def _rms_norm(x, g):
    return x * lax.rsqrt(jnp.mean(x * x, axis=-1, keepdims=True) + LN_EPS) * g


def _in_proj_kernel(*refs, apply_ln):
    if apply_ln:
        (x_ref, lng_ref, lnb_ref, wa_ref, wt_ref, gbr_ref, gbc_ref, qg_ref, wq_ref, kvg_ref, wuk_ref, wuvt_ref,
         cq_ref, sq_ref, ck_ref, sk_ref,
         xn_ref, q_ref, k_ref, vt_ref, pool_ref, km_ref, om_ref, gc_ref, qmt_ref, vmt_ref, gt_ref) = refs
    else:
        (x_ref, wa_ref, wt_ref, gbr_ref, gbc_ref, qg_ref, wq_ref, kvg_ref, wuk_ref, wuvt_ref,
         cq_ref, sq_ref, ck_ref, sk_ref,
         q_ref, k_ref, vt_ref, pool_ref, km_ref, om_ref, gc_ref, qmt_ref, vmt_ref, gt_ref) = refs
    x = x_ref[...]
    if apply_ln:
        x = _layer_norm(x, lng_ref[...], lnb_ref[...])
        xn_ref[...] = x
    xb = x.astype(BF16)
    pa = jnp.dot(xb, wa_ref[...], preferred_element_type=F32)
    pt = lax.dot_general(wt_ref[...], xb, _NT, preferred_element_type=F32)
    qmt_ref[...] = pt[:ML_WIDTH].astype(BF16)
    vrow = lax.broadcasted_iota(jnp.int32, (ML_VT_ROWS, 1), 0) % ML_VT_HEAD
    vmt_ref[...] = (pt[ML_WIDTH:ML_WIDTH + ML_VT_ROWS] + (vrow == ML_DIM).astype(F32)).astype(BF16)
    gt_ref[...] = pt[ML_WIDTH + ML_VT_ROWS:] + gbr_ref[...]

    pool_ref[...] = pa[:, _PA_POOL:_PA_KM]
    km_ref[...] = pa[:, _PA_KM:_PA_OM].astype(BF16)
    om_ref[...] = pa[:, _PA_OM:_PA_GC]
    gc_ref[...] = pa[:, _PA_GC:_PA_END] + gbc_ref[...]

    cqn = _rms_norm(pa[:, _PA_CQ:_PA_CKV], qg_ref[...]).astype(BF16)
    q2t = lax.dot_general(wq_ref[...], cqn, _NT, preferred_element_type=F32)
    cq, sq = cq_ref[...], sq_ref[...]
    hw = N_HEADS * D_HEAD_PAD
    for h in range(N_HEADS):
        lo = h * D_HEAD_PAD
        qh = q2t[lo:lo + D_HEAD_PAD] * cq + q2t[hw + lo:hw + lo + D_HEAD_PAD] * sq
        q_ref[lo:lo + D_HEAD_PAD, :] = qh.astype(BF16)

    ckvn = _rms_norm(pa[:, _PA_CKV:_PA_KR], kvg_ref[...]).astype(BF16)
    kn = jnp.dot(ckvn, wuk_ref[...], preferred_element_type=F32)
    kr = pa[:, _PA_KR:_PA_KRS] * ck_ref[...] + pa[:, _PA_KRS:_PA_POOL] * sk_ref[...]
    for h in range(N_HEADS):
        lo = h * D_HEAD_PAD
        k_ref[:, lo:lo + D_HEAD_PAD] = (kn[:, lo:lo + D_HEAD_PAD] + kr).astype(BF16)
    vt = lax.dot_general(wuvt_ref[...], ckvn, _NT, preferred_element_type=F32)
    vt_ref[...] = vt.astype(BF16)


def _const_spec(shape):
    nd = len(shape)
    return pl.BlockSpec(shape, lambda *_: (0,) * nd)


def _in_proj(x, lw, tabs, seq, *, ln=None, tm=256):
    t_tokens = x.shape[0]
    n_seq_tiles = seq // tm
    apply_ln = ln is not None
    row = lambda w: pl.BlockSpec((tm, w), lambda i: (i, 0))
    colt = lambda r: pl.BlockSpec((r, tm), lambda i: (0, i))
    tab = pl.BlockSpec((tm, D_HEAD_PAD), lambda i: (i % n_seq_tiles, 0))
    consts = [lw["wa"], lw["wt"], lw["gate_bias_col"], lw["gate_bias_row"], lw["q_norm_g"], lw["wq2"],
              lw["kv_norm_g"], lw["wuk"], lw["wuvt"]]
    in_specs = [row(D_MODEL)]
    args = [x]
    if apply_ln:
        in_specs += [_const_spec((1, D_MODEL))] * 2
        args += [ln[0], ln[1]]
    tab_t = pl.BlockSpec((D_HEAD_PAD, tm), lambda i: (0, i % n_seq_tiles))
    in_specs += [_const_spec(c.shape) for c in consts] + [tab_t, tab_t, tab, tab]
    args += consts + list(tabs)
    hw = N_HEADS * D_HEAD_PAD
    out_shape, out_specs = [], []
    if apply_ln:
        out_shape.append(jax.ShapeDtypeStruct((t_tokens, D_MODEL), F32))
        out_specs.append(row(D_MODEL))
    out_shape += [
        jax.ShapeDtypeStruct((hw, t_tokens), BF16),
        jax.ShapeDtypeStruct((t_tokens, hw), BF16),
        jax.ShapeDtypeStruct((N_HEADS * D_V, t_tokens), BF16),
        jax.ShapeDtypeStruct((t_tokens, POOL_WIDTH), F32),
        jax.ShapeDtypeStruct((t_tokens, ML_WIDTH), BF16),
        jax.ShapeDtypeStruct((t_tokens, ML_WIDTH), F32),
        jax.ShapeDtypeStruct((t_tokens, D_HEAD_PAD), F32),
        jax.ShapeDtypeStruct((ML_WIDTH, t_tokens), BF16),
        jax.ShapeDtypeStruct((ML_VT_ROWS, t_tokens), BF16),
        jax.ShapeDtypeStruct((N_GATES, t_tokens), F32),
    ]
    out_specs += [colt(hw), row(hw), colt(N_HEADS * D_V), row(POOL_WIDTH), row(ML_WIDTH), row(ML_WIDTH),
                  row(D_HEAD_PAD), colt(ML_WIDTH), colt(ML_VT_ROWS), colt(N_GATES)]
    return pl.pallas_call(
        functools.partial(_in_proj_kernel, apply_ln=apply_ln),
        grid=(t_tokens // tm,),
        in_specs=in_specs,
        out_specs=out_specs,
        out_shape=out_shape,
        compiler_params=pltpu.CompilerParams(dimension_semantics=("parallel",),
                                             vmem_limit_bytes=VMEM_LIMIT),
        name="in_proj_ln" if apply_ln else "in_proj",
    )(*args)


def _attention_kernel(q_ref, k_ref, vt_ref, o_ref, s_ref, smax_ref, acc_ref, m_ref, *, tk, sub, unroll):
    tq = q_ref.shape[1]
    n_sub = tq // sub
    n_k = k_ref.shape[0] // tk
    assert unroll % 2 == 0

    def scores(t, c):
        kblk = k_ref[pl.ds(pl.multiple_of(t * tk, tk), tk), :]
        return jnp.dot(kblk, q_ref[:, c * sub:(c + 1) * sub], preferred_element_type=F32)

    def step(t, parity, with_next):
        vblk = vt_ref[:, pl.ds(pl.multiple_of(t * tk, tk), tk)]
        vext = jnp.concatenate([vblk, jnp.ones((BF16_ROWS, tk), BF16)], axis=0)
        for c in range(n_sub):
            cols = slice(c * sub, (c + 1) * sub)
            if with_next:
                s_next = scores(t + 1, c)
                s_ref[1 - parity, c] = s_next
                smax_ref[1 - parity, :, cols] = jnp.max(s_next, axis=0, keepdims=True)
            s = s_ref[parity, c]
            m = m_ref[:, cols]
            m_new = jnp.maximum(m, smax_ref[parity, :, cols])
            alpha = jnp.exp2(m - m_new)
            p = jnp.exp2(s - m_new).astype(BF16)
            pv = jnp.dot(vext, p, preferred_element_type=F32)
            acc_ref[:, cols] = alpha * acc_ref[:, cols] + pv[:ACC_ROWS]
            m_ref[:, cols] = m_new

    def body(j, _):
        for i in range(unroll):
            step(unroll * j + i, i % 2, True)
        return 0

    m_ref[...] = jnp.full(m_ref.shape, NEG_BIG, F32)
    acc_ref[...] = jnp.zeros_like(acc_ref)
    for c in range(n_sub):
        s0 = scores(0, c)
        s_ref[0, c] = s0
        smax_ref[0, :, c * sub:(c + 1) * sub] = jnp.max(s0, axis=0, keepdims=True)
    trips = (n_k - 1) // unroll
    if trips:
        lax.fori_loop(0, trips, body, 0)
    for t in range(trips * unroll, n_k):
        step(t, t % 2, t + 1 < n_k)
    o_ref[...] = (acc_ref[:D_V, :] / acc_ref[D_V:D_V + 1, :]).astype(BF16)


def _attention(q, k, vt, batch, seq, *, tq=1024, tk=512, sub=256, unroll=4):
    n_q = seq // tq
    t_tokens = batch * seq
    k3 = k.reshape(batch, seq, N_HEADS * D_HEAD_PAD)
    return pl.pallas_call(
        functools.partial(_attention_kernel, tk=tk, sub=sub, unroll=unroll),
        grid=(batch, N_HEADS, n_q),
        in_specs=[
            pl.BlockSpec((D_HEAD_PAD, tq), lambda b, h, i: (h, b * n_q + i)),
            pl.BlockSpec((None, seq, D_HEAD_PAD), lambda b, h, i: (b, 0, h)),
            pl.BlockSpec((D_V, seq), lambda b, h, i: (h, b)),
        ],
        out_specs=pl.BlockSpec((D_V, tq), lambda b, h, i: (h, b * n_q + i)),
        out_shape=jax.ShapeDtypeStruct((N_HEADS * D_V, t_tokens), BF16),
        scratch_shapes=[pltpu.VMEM((2, tq // sub, tk, sub), F32), pltpu.VMEM((2, 1, tq), F32),
                        pltpu.VMEM((ACC_ROWS, tq), F32), pltpu.VMEM((1, tq), F32)],
        compiler_params=pltpu.CompilerParams(
            dimension_semantics=("parallel", "parallel", "arbitrary"), vmem_limit_bytes=VMEM_LIMIT),
        name="attention",
    )(q, k3, vt)


def _pool_kernel(x_ref, prev_ref, next_ref, w_ref, scale_ref, o_ref, *, seq):
    i = pl.program_id(1)
    ts = x_ref.shape[0]
    x = x_ref[...]
    prev = jnp.where(i == 0, 0.0, prev_ref[...])
    nxt = jnp.where(i == pl.num_programs(1) - 1, 0.0, next_ref[...])
    xe = jnp.concatenate([prev, x, nxt], axis=0)
    n = ts + 2 * POOL_HALO
    p2 = xe[0:n - 1] + xe[1:n]
    p4 = p2[0:n - 3] + p2[2:n - 1]
    p8 = p4[0:n - 7] + p4[4:n - 3]
    p16 = p8[0:n - 15] + p8[8:n - 7]
    sums = {2: p2, 4: p4, 8: p8, 16: p16}
    t = i * ts + lax.broadcasted_iota(jnp.int32, (ts, 1), 0)
    lane = lax.broadcasted_iota(jnp.int32, (1, POOL_WIDTH), 1)
    y = jnp.zeros((ts, POOL_WIDTH), F32)
    for g, w in enumerate(POOL_WINDOWS):
        start = POOL_HALO - w // 2
        win = sums[w][start:start + ts]
        cnt = (jnp.minimum(t + w // 2, seq) - jnp.maximum(t - w // 2, 0)).astype(F32)
        y = jnp.where(lane // POOL_GROUP == g, win / cnt - x, y)
    out = jnp.dot(y.astype(BF16), w_ref[...], preferred_element_type=F32) * scale_ref[...]
    o_ref[...] = out.astype(BF16)


def _pool(xp, w_bd, scale, batch, seq, *, ts=512):
    n_t = seq // ts
    hb = ts // POOL_HALO
    x3 = xp.reshape(batch, seq, POOL_WIDTH)
    out = pl.pallas_call(
        functools.partial(_pool_kernel, seq=seq),
        grid=(batch, n_t),
        in_specs=[
            pl.BlockSpec((None, ts, POOL_WIDTH), lambda b, i: (b, i, 0)),
            pl.BlockSpec((None, POOL_HALO, POOL_WIDTH), lambda b, i: (b, jnp.maximum(i * hb - 1, 0), 0)),
            pl.BlockSpec((None, POOL_HALO, POOL_WIDTH),
                         lambda b, i: (b, jnp.minimum((i + 1) * hb, seq // POOL_HALO - 1), 0)),
            _const_spec((POOL_WIDTH, POOL_WIDTH)),
            _const_spec((1, POOL_WIDTH)),
        ],
        out_specs=pl.BlockSpec((None, ts, POOL_WIDTH), lambda b, i: (b, i, 0)),
        out_shape=jax.ShapeDtypeStruct((batch, seq, POOL_WIDTH), BF16),
        compiler_params=pltpu.CompilerParams(dimension_semantics=("parallel", "parallel")),
        name="pool",
    )(x3, x3, x3, w_bd, scale)
    return out.reshape(batch * seq, POOL_WIDTH)


def _log_sigmoid(x):
    return -(jnp.maximum(-x, 0.0) + jnp.log1p(jnp.exp(-jnp.abs(x))))


_AUX_KINDS = 5
_AUX_ROWS = _AUX_KINDS * 8


def _mlstm_intra_kernel(qt_ref, k_ref, vt_ref, gt_ref, gc_ref, af_ref, ab_ref, aux_ref, u_ref, *, chunks):
    L = ML_CHUNK
    hi = lax.Precision.HIGHEST
    r = lax.broadcasted_iota(jnp.int32, (L, L), 0)
    c = lax.broadcasted_iota(jnp.int32, (L, L), 1)
    lower, upper = c <= r, c >= r
    lo_f, up_f = lower.astype(F32), upper.astype(F32)
    lane_k = lax.broadcasted_iota(jnp.int32, (L, ML_WIDTH), 1)
    lane = lax.broadcasted_iota(jnp.int32, (ML_VT_HEAD, L), 1)
    aux_ref[...] = jnp.zeros_like(aux_ref)
    for ci in range(chunks):
        sl = slice(ci * L, (ci + 1) * L)
        g_rows = gt_ref[:, sl]
        g_cols = gc_ref[sl, :]
        logf_rows = _log_sigmoid(g_rows)
        logf_cols = _log_sigmoid(g_cols)
        qt = qt_ref[:, sl]
        k = k_ref[sl, :]
        b_rows = (jnp.dot(logf_rows, up_f, precision=hi, preferred_element_type=F32),
                  jnp.dot(logf_rows, lo_f, precision=hi, preferred_element_type=F32))
        b_cols = (jnp.dot(lo_f, logf_cols, precision=hi, preferred_element_type=F32),
                  jnp.dot(up_f, logf_cols, precision=hi, preferred_element_type=F32))
        u_out = [[None] * ML_HEADS for _ in range(2)]
        for h in range(ML_HEADS):
            kh = jnp.where(lane_k // ML_DIM == h, k, jnp.zeros_like(k))
            sc_t = jnp.dot(kh, qt, preferred_element_type=F32)
            vth = vt_ref[h * ML_VT_HEAD:(h + 1) * ML_VT_HEAD, sl]
            k_pair = k[:, (h // 2) * L:(h // 2 + 1) * L]
            for d in range(2):
                gi, gf = d * 8 + h, d * 8 + ML_HEADS + h
                valid = upper if d == 0 else lower
                b_row = b_rows[d][gf:gf + 1]
                bv_col = g_cols[:, gi:gi + 1] - b_cols[d][:, gf:gf + 1]
                dm_t = jnp.where(valid, b_row + bv_col, NEG_BIG)
                dmax = jnp.max(dm_t, axis=0, keepdims=True)
                sp_t = (sc_t * jnp.exp(dm_t - dmax)).astype(BF16)
                a_ext = jnp.dot(vth, sp_t, preferred_element_type=F32)
                a_ref = af_ref if d == 0 else ab_ref
                a_ref[h * ML_DIM:(h + 1) * ML_DIM, sl] = a_ext[:ML_DIM]
                b_last = b_row[:, L - 1:L] if d == 0 else b_row[:, 0:1]
                g_row = b_last + (g_rows[gi:gi + 1] - b_row)
                gmax = jnp.max(g_row, axis=1, keepdims=True)
                ws = jnp.exp(g_row - gmax)
                vw = (vth.astype(F32) * ws).astype(BF16)
                u_out[d][h] = jnp.dot(vw, k_pair, preferred_element_type=F32)
                base = d * _AUX_ROWS + h
                aux_ref[base:base + 1, sl] = b_row
                aux_ref[base + 8:base + 9, sl] = dmax
                aux_ref[base + 16:base + 17, sl] = a_ext[ML_DIM:ML_DIM + 1]
                aux_ref[base + 24:base + 25, sl] = jnp.broadcast_to(b_last, (1, L))
                aux_ref[base + 32:base + 33, sl] = jnp.broadcast_to(gmax, (1, L))
        for d in range(2):
            for p in range(ML_HEADS // 2):
                u_ref[ci, d, p] = jnp.where(lane < ML_DIM, u_out[d][2 * p], u_out[d][2 * p + 1])


def _mlstm_intra(qmt, km, vmt, gt, gc, *, chunks=4):
    L = ML_CHUNK
    t_tokens = km.shape[0]
    w = chunks * L
    col = lambda rows: pl.BlockSpec((rows, w), lambda i: (0, i))
    return pl.pallas_call(
        functools.partial(_mlstm_intra_kernel, chunks=chunks),
        grid=(t_tokens // w,),
        in_specs=[col(ML_WIDTH), pl.BlockSpec((w, ML_WIDTH), lambda i: (i, 0)), col(ML_VT_ROWS), col(N_GATES),
                  pl.BlockSpec((w, D_HEAD_PAD), lambda i: (i, 0))],
        out_specs=[col(ML_WIDTH), col(ML_WIDTH), col(2 * _AUX_ROWS),
                   pl.BlockSpec((chunks, 2, ML_HEADS // 2, ML_VT_HEAD, L), lambda i: (i, 0, 0, 0, 0))],
        out_shape=[jax.ShapeDtypeStruct((ML_WIDTH, t_tokens), F32),
                   jax.ShapeDtypeStruct((ML_WIDTH, t_tokens), F32),
                   jax.ShapeDtypeStruct((2 * _AUX_ROWS, t_tokens), F32),
                   jax.ShapeDtypeStruct((t_tokens // L, 2, ML_HEADS // 2, ML_VT_HEAD, L), F32)],
        compiler_params=pltpu.CompilerParams(dimension_semantics=("parallel",)),
        name="mlstm_intra",
    )(qmt, km, vmt, gt, gc)


def _mlstm_scan_kernel(qtf_ref, qtb_ref, af_ref, ab_ref, auxf_ref, auxb_ref, uf_ref, ub_ref,
                       hf_ref, hb_ref, c_ref, m_ref):
    L = ML_CHUNK

    @pl.when(pl.program_id(1) == 0)
    def _():
        c_ref[...] = jnp.zeros_like(c_ref)
        m_ref[...] = jnp.zeros_like(m_ref)

    lane = lax.broadcasted_iota(jnp.int32, (ML_VT_HEAD, L), 1)
    lane_row = lax.broadcasted_iota(jnp.int32, (1, L), 1)
    dirs = ((qtf_ref, af_ref, auxf_ref, uf_ref, hf_ref), (qtb_ref, ab_ref, auxb_ref, ub_ref, hb_ref))
    for d, (qt_ref, a_ref, aux_ref, u_ref, h_ref) in enumerate(dirs):
        base = d * _AUX_ROWS
        b = aux_ref[base:base + 8, :]
        dmax = aux_ref[base + 8:base + 16, :]
        den_intra = aux_ref[base + 16:base + 24, :]
        b_last = aux_ref[base + 24:base + 32, :]
        gmax = aux_ref[base + 32:base + 40, :]
        m_old = m_ref[d]
        m_inter = b + m_old
        m_j = jnp.maximum(m_inter, dmax)
        e_inter = jnp.exp(m_inter - m_j)
        e_intra = jnp.exp(dmax - m_j)
        e_floor = jnp.exp(-m_j)
        m_new = jnp.maximum(b_last + m_old, gmax)
        decay = jnp.exp(b_last + m_old - m_new)
        gain = jnp.exp(gmax - m_new)
        m_ref[d] = m_new
        for p in range(ML_HEADS // 2):
            cp = c_ref[d, p]
            qtp = qt_ref[p * L:(p + 1) * L, :]
            for hh in range(2):
                h = 2 * p + hh
                own = lane < ML_DIM if hh == 0 else lane >= ML_DIM
                cm = jnp.where(own, cp, 0.0).astype(BF16)
                x_t = jnp.dot(cm, qtp, preferred_element_type=F32)
                num = e_inter[h:h + 1] * x_t[:ML_DIM] + e_intra[h:h + 1] * a_ref[h * ML_DIM:(h + 1) * ML_DIM, :]
                den = e_inter[h:h + 1] * x_t[ML_DIM:ML_DIM + 1] + e_intra[h:h + 1] * den_intra[h:h + 1]
                h_ref[h * ML_DIM:(h + 1) * ML_DIM, :] = num / jnp.maximum(jnp.abs(den), e_floor[h:h + 1])
            lo, hi_ = 2 * p, 2 * p + 1
            decay_p = jnp.where(lane_row < ML_DIM, decay[lo:lo + 1], decay[hi_:hi_ + 1])
            gain_p = jnp.where(lane_row < ML_DIM, gain[lo:lo + 1], gain[hi_:hi_ + 1])
            c_ref[d, p] = decay_p * cp + gain_p * u_ref[p]


def _mlstm_scan(qmt, a_f, a_b, aux, u, batch, seq):
    L = ML_CHUNK
    nc = seq // L
    t_tokens = batch * seq
    fwd = lambda b, c: b * nc + c
    bwd = lambda b, c: b * nc + (nc - 1 - c)
    col = lambda f, rows: pl.BlockSpec((rows, L), lambda b, c: (0, f(b, c)))
    ublk = lambda f, d: pl.BlockSpec((None, None, ML_HEADS // 2, ML_VT_HEAD, L),
                                     lambda b, c: (f(b, c), d, 0, 0, 0))
    return pl.pallas_call(
        _mlstm_scan_kernel,
        grid=(batch, nc),
        in_specs=[col(fwd, ML_WIDTH), col(bwd, ML_WIDTH), col(fwd, ML_WIDTH), col(bwd, ML_WIDTH),
                  col(fwd, 2 * _AUX_ROWS), col(bwd, 2 * _AUX_ROWS), ublk(fwd, 0), ublk(bwd, 1)],
        out_specs=[col(fwd, ML_WIDTH), col(bwd, ML_WIDTH)],
        out_shape=[jax.ShapeDtypeStruct((ML_WIDTH, t_tokens), F32)] * 2,
        scratch_shapes=[pltpu.VMEM((2, ML_HEADS // 2, ML_VT_HEAD, L), F32), pltpu.VMEM((2, 8, L), F32)],
        compiler_params=pltpu.CompilerParams(dimension_semantics=("parallel", "arbitrary")),
        name="mlstm_scan",
    )(qmt, qmt, a_f, a_b, aux, aux, u, u)


def _out_ffn_kernel(x_ref, yat_ref, yp_ref, hf_ref, hb_ref, om_ref, ng_ref,
                    woa_ref, wop_ref, wom_ref, l1g_ref, l1b_ref, wg_ref, wu_ref, wd_ref,
                    l2g_ref, l2b_ref, o_ref):
    x = x_ref[...]
    parts = []
    for h in range(ML_HEADS):
        rows = slice(h * ML_DIM, (h + 1) * ML_DIM)
        ht = hf_ref[rows, :] + hb_ref[rows, :]
        hc = ht - jnp.mean(ht, axis=0, keepdims=True)
        var = jnp.mean(hc * hc, axis=0, keepdims=True)
        parts.append(hc * lax.rsqrt(var + LN_EPS))
    hn = jnp.concatenate(parts, axis=0).T * ng_ref[...]
    y_ml = (hn / (1.0 + jnp.exp(-om_ref[...]))).astype(BF16)
    y_at = yat_ref[...].T
    mix = (jnp.dot(y_at, woa_ref[...], preferred_element_type=F32)
           + jnp.dot(yp_ref[...], wop_ref[...], preferred_element_type=F32)
           + jnp.dot(y_ml, wom_ref[...], preferred_element_type=F32))
    x1 = _layer_norm(ALPHA * x + mix, l1g_ref[...], l1b_ref[...])
    x1b = x1.astype(BF16)
    gate = jnp.dot(x1b, wg_ref[...], preferred_element_type=F32)
    up = jnp.dot(x1b, wu_ref[...], preferred_element_type=F32)
    hid = (gate / (1.0 + jnp.exp(-gate)) * up).astype(BF16)
    ffn = jnp.dot(hid, wd_ref[...], preferred_element_type=F32)
    o_ref[...] = _layer_norm(ALPHA * x1 + ffn, l2g_ref[...], l2b_ref[...])


def _out_ffn(x, yat, yp, hf, hb, om, lw, *, tm=256):
    t_tokens = x.shape[0]
    row = lambda w: pl.BlockSpec((tm, w), lambda i: (i, 0))
    colt = lambda r: pl.BlockSpec((r, tm), lambda i: (0, i))
    single = lambda a: pl.BlockSpec(a.shape, lambda i: (0,) * a.ndim, pipeline_mode=pl.Buffered(1))
    consts = [lw["norm_g"], lw["wo_a"], lw["wo_p"], lw["wo_m"], lw["ln1_g"], lw["ln1_b"],
              lw["w_gate"], lw["w_up"], lw["w_down"], lw["ln2_g"], lw["ln2_b"]]
    return pl.pallas_call(
        _out_ffn_kernel,
        grid=(t_tokens // tm,),
        in_specs=[row(D_MODEL), colt(N_HEADS * D_V), row(POOL_WIDTH),
                  colt(ML_WIDTH), colt(ML_WIDTH), row(ML_WIDTH)] + [single(c) for c in consts],
        out_specs=row(D_MODEL),
        out_shape=jax.ShapeDtypeStruct((t_tokens, D_MODEL), F32),
        compiler_params=pltpu.CompilerParams(dimension_semantics=("parallel",),
                                             vmem_limit_bytes=VMEM_LIMIT),
        name="out_ffn",
    )(x, yat, yp, hf, hb, om, *consts)


def _head_pad_cols(w, n_heads, width, dst_lo):
    k = w.shape[0]
    w3 = w.reshape(k, n_heads, width)
    out = jnp.zeros((k, n_heads, D_HEAD_PAD), w.dtype)
    out = out.at[:, :, dst_lo:dst_lo + width].set(w3)
    return out.reshape(k, n_heads * D_HEAD_PAD)


def _rope_partner(w_rope):
    half = D_ROPE // 2
    return jnp.concatenate([-w_rope[..., half:], w_rope[..., :half]], axis=-1)


def _prep_layer_weights(l, w_in, q_norm_g, w_uq, kv_norm_g, w_ukv, w_pool, pool_scale, gate_bias,
                        norm_g, w_out, ln1_g, ln1_b, w_gate, w_up, w_down, ln2_g, ln2_b):
    wi = w_in[l]
    c = np.cumsum((0, 256, 128, 32, 256, 256, 256, 256, 256, 16))
    w_cq, w_ckv, w_kr, w_pl, w_qm, w_km, w_vm, w_om, w_g = (wi[:, c[j]:c[j + 1]] for j in range(9))
    kr_pad = jnp.zeros((D_MODEL, D_HEAD_PAD), F32).at[:, D_NOPE:D_NOPE + D_ROPE].set(w_kr)
    krs_pad = jnp.zeros((D_MODEL, D_HEAD_PAD), F32).at[:, D_NOPE:D_NOPE + D_ROPE].set(_rope_partner(w_kr))
    g_pad = jnp.zeros((D_MODEL, D_HEAD_PAD), F32).at[:, :N_GATES].set(w_g)
    wa = jnp.concatenate([w_cq, w_ckv, kr_pad, krs_pad, w_pl, w_km * ML_DIM ** -0.5, w_om, g_pad],
                         axis=1).astype(BF16)
    vm_t = jnp.zeros((ML_HEADS, ML_VT_HEAD, D_MODEL), F32).at[:, :ML_DIM, :].set(
        w_vm.T.reshape(ML_HEADS, ML_DIM, D_MODEL)).reshape(ML_VT_ROWS, D_MODEL)
    wt = jnp.concatenate([w_qm.T, vm_t, w_g.T], axis=0).astype(BF16)
    bias_row = jnp.zeros((1, D_HEAD_PAD), F32).at[0, :N_GATES].set(gate_bias[l])
    uq = w_uq[l].reshape(Q_RANK, N_HEADS, D_NOPE + D_ROPE)
    uq_plain = jnp.zeros((Q_RANK, N_HEADS, D_HEAD_PAD), F32).at[:, :, :D_NOPE + D_ROPE].set(uq)
    uq_part = jnp.zeros((Q_RANK, N_HEADS, D_HEAD_PAD), F32).at[:, :, D_NOPE:D_NOPE + D_ROPE].set(
        _rope_partner(uq[:, :, D_NOPE:]))
    wq2 = jnp.concatenate([uq_plain.reshape(Q_RANK, -1), uq_part.reshape(Q_RANK, -1)], axis=1).T.astype(BF16)
    ukv = w_ukv[l].reshape(KV_RANK, N_HEADS, D_NOPE + D_V)
    wuk = _head_pad_cols(ukv[:, :, :D_NOPE].reshape(KV_RANK, -1), N_HEADS, D_NOPE, 0).astype(BF16)
    wuvt = ukv[:, :, D_NOPE:].reshape(KV_RANK, -1).T.astype(BF16)
    w_bd = jnp.zeros((POOL_WIDTH, POOL_WIDTH), F32)
    for g in range(len(POOL_WINDOWS)):
        s = slice(g * POOL_GROUP, (g + 1) * POOL_GROUP)
        w_bd = w_bd.at[s, s].set(w_pool[l, g])
    wo = w_out[l].astype(BF16)
    a_end = N_HEADS * D_V
    return {
        "wa": wa, "wt": wt, "gate_bias_col": gate_bias[l].reshape(N_GATES, 1), "gate_bias_row": bias_row,
        "q_norm_g": q_norm_g[l].reshape(1, -1), "wq2": wq2, "kv_norm_g": kv_norm_g[l].reshape(1, -1),
        "wuk": wuk, "wuvt": wuvt, "w_pool": w_bd.astype(BF16), "pool_scale": pool_scale[l].reshape(1, -1),
        "norm_g": norm_g[l].reshape(1, -1),
        "wo_a": wo[:a_end], "wo_p": wo[a_end:a_end + POOL_WIDTH], "wo_m": wo[a_end + POOL_WIDTH:],
        "ln1_g": ln1_g[l].reshape(1, -1), "ln1_b": ln1_b[l].reshape(1, -1),
        "w_gate": w_gate[l].astype(BF16), "w_up": w_up[l].astype(BF16), "w_down": w_down[l].astype(BF16),
        "ln2_g": ln2_g[l].reshape(1, -1), "ln2_b": ln2_b[l].reshape(1, -1),
    }


def _rope_tables(seq):
    half = D_ROPE // 2
    inv = 1.0 / (ROPE_THETA ** (jnp.arange(0, D_ROPE, 2, dtype=F32) / D_ROPE))
    ang = jnp.arange(seq, dtype=F32)[:, None] * inv[None, :]
    cos, sin = jnp.cos(ang), jnp.sin(ang)
    one = jnp.ones((seq, D_NOPE), F32)
    zero_n = jnp.zeros((seq, D_NOPE), F32)
    zero_p = jnp.zeros((seq, D_HEAD_PAD - D_NOPE - D_ROPE), F32)
    q_scale = (D_NOPE + D_ROPE) ** -0.5 * LOG2E
    cq = jnp.concatenate([one, cos, cos, zero_p], axis=1) * q_scale
    sq = jnp.concatenate([zero_n, sin, sin, zero_p], axis=1) * q_scale
    ck = jnp.concatenate([zero_n, cos, cos, zero_p], axis=1)
    sk = jnp.concatenate([zero_n, sin, sin, zero_p], axis=1)
    return cq.T, sq.T, ck, sk


def _trunk(x, ln_in, layers):
    batch, seq, _ = x.shape
    tabs = _rope_tables(seq)
    xf = x.reshape(batch * seq, D_MODEL)
    for l, lw in enumerate(layers):
        outs = _in_proj(xf, lw, tabs, seq, ln=ln_in if l == 0 else None)
        if l == 0:
            xf, outs = outs[0], outs[1:]
        q, k, vt, xp, km, om, gc, qmt, vmt, gt = outs
        yat = _attention(q, k, vt, batch, seq, unroll=6)
        yp = _pool(xp, lw["w_pool"], lw["pool_scale"], batch, seq)
        a_f, a_b, aux, u = _mlstm_intra(qmt, km, vmt, gt, gc)
        hf, hb = _mlstm_scan(qmt, a_f, a_b, aux, u, batch, seq)
        xf = _out_ffn(xf, yat, yp, hf, hb, om, lw)
    return xf.reshape(batch, seq, D_MODEL)


def kernel(x_prompt, x_sample, ln_in_g, ln_in_b, w_in, q_norm_g, w_uq, kv_norm_g, w_ukv, w_pool, pool_scale,
           mlstm_gate_bias, mlstm_norm_g, w_out, ln1_g, ln1_b, w_gate, w_up, w_down, ln2_g, ln2_b):
    layers = [_prep_layer_weights(l, w_in, q_norm_g, w_uq, kv_norm_g, w_ukv, w_pool, pool_scale,
                                  mlstm_gate_bias, mlstm_norm_g, w_out, ln1_g, ln1_b, w_gate, w_up, w_down,
                                  ln2_g, ln2_b) for l in range(DEPTH)]
    ln_in = (ln_in_g.reshape(1, -1), ln_in_b.reshape(1, -1))
    return (_trunk(x_prompt, ln_in, layers), _trunk(x_sample, ln_in, layers))
```

```python
import functools
import math

import jax
import jax.numpy as jnp
import numpy as np
from jax import lax
from jax.experimental import pallas as pl
from jax.experimental.pallas import tpu as pltpu

F32 = jnp.float32
BF16 = jnp.bfloat16

D_MODEL = 1024
DEPTH = 4
N_HEADS = 8
D_NOPE = 64
D_ROPE = 32
D_V = 64
BF16_ROWS = 16
ACC_ROWS = 72
Q_RANK = 256
KV_RANK = 128
ROPE_THETA = 10000.0
D_HEAD_PAD = 128
POOL_WIDTH = 256
POOL_WINDOWS = (2, 4, 8, 16)
POOL_GROUP = 64
POOL_HALO = 8
ML_HEADS = 4
ML_DIM = 64
ML_WIDTH = 256
ML_CHUNK = 128
ML_VT_HEAD = 80
ML_VT_ROWS = ML_HEADS * ML_VT_HEAD
N_GATES = 16
NEG_BIG = -1e30
D_FF = 2816
ALPHA = (2 * DEPTH) ** 0.25
LN_EPS = 1e-5
LOG2E = 1.4426950408889634

VMEM_LIMIT = 56 * 1024 * 1024

_PA_CQ, _PA_CKV, _PA_KR, _PA_KRS, _PA_POOL, _PA_KM, _PA_OM, _PA_END = (
    0, 256, 384, 512, 640, 896, 1152, 1408)

_NT = (((1,), (1,)), ((), ()))


def _layer_norm(x, g, b):
    mu = jnp.mean(x, axis=-1, keepdims=True)
    xc = x - mu
    var = jnp.mean(xc * xc, axis=-1, keepdims=True)
    return xc * lax.rsqrt(var + LN_EPS) * g + b


def _rms_norm(x, g):
    return x * lax.rsqrt(jnp.mean(x * x, axis=-1, keepdims=True) + LN_EPS) * g


def _in_proj_kernel(*refs, apply_ln):
    if apply_ln:
        (x_ref, lng_ref, lnb_ref, wa_ref, wt_ref, gb_ref, qg_ref, wq_ref, kvg_ref, wuk_ref, wuvt_ref,
         cq_ref, sq_ref, ck_ref, sk_ref,
         xn_ref, q_ref, k_ref, vt_ref, pool_ref, km_ref, om_ref, qmt_ref, vmt_ref, gt_ref) = refs
    else:
        (x_ref, wa_ref, wt_ref, gb_ref, qg_ref, wq_ref, kvg_ref, wuk_ref, wuvt_ref,
         cq_ref, sq_ref, ck_ref, sk_ref,
         q_ref, k_ref, vt_ref, pool_ref, km_ref, om_ref, qmt_ref, vmt_ref, gt_ref) = refs
    x = x_ref[...]
    if apply_ln:
        x = _layer_norm(x, lng_ref[...], lnb_ref[...])
        xn_ref[...] = x
    xb = x.astype(BF16)
    pa = jnp.dot(xb, wa_ref[...], preferred_element_type=F32)
    pt = lax.dot_general(wt_ref[...], xb, _NT, preferred_element_type=F32)
    qmt_ref[...] = pt[:ML_WIDTH].astype(BF16)
    vrow = lax.broadcasted_iota(jnp.int32, (ML_VT_ROWS, 1), 0) % ML_VT_HEAD
    vmt_ref[...] = (pt[ML_WIDTH:ML_WIDTH + ML_VT_ROWS] + (vrow == ML_DIM).astype(F32)).astype(BF16)
    gt_ref[...] = pt[ML_WIDTH + ML_VT_ROWS:] + gb_ref[...]

    pool_ref[...] = pa[:, _PA_POOL:_PA_KM]
    km_ref[...] = pa[:, _PA_KM:_PA_OM].astype(BF16)
    om_ref[...] = pa[:, _PA_OM:_PA_END]

    cqn = _rms_norm(pa[:, _PA_CQ:_PA_CKV], qg_ref[...]).astype(BF16)
    q2t = lax.dot_general(wq_ref[...], cqn, _NT, preferred_element_type=F32)
    cq, sq = cq_ref[...], sq_ref[...]
    hw = N_HEADS * D_HEAD_PAD
    for h in range(N_HEADS):
        lo = h * D_HEAD_PAD
        qh = q2t[lo:lo + D_HEAD_PAD] * cq + q2t[hw + lo:hw + lo + D_HEAD_PAD] * sq
        q_ref[lo:lo + D_HEAD_PAD, :] = qh.astype(BF16)

    ckvn = _rms_norm(pa[:, _PA_CKV:_PA_KR], kvg_ref[...]).astype(BF16)
    kn = jnp.dot(ckvn, wuk_ref[...], preferred_element_type=F32)
    kr = pa[:, _PA_KR:_PA_KRS] * ck_ref[...] + pa[:, _PA_KRS:_PA_POOL] * sk_ref[...]
    for h in range(N_HEADS):
        lo = h * D_HEAD_PAD
        k_ref[:, lo:lo + D_HEAD_PAD] = (kn[:, lo:lo + D_HEAD_PAD] + kr).astype(BF16)
    vt = lax.dot_general(wuvt_ref[...], ckvn, _NT, preferred_element_type=F32)
    vt_ref[...] = vt.astype(BF16)


def _const_spec(shape):
    nd = len(shape)
    return pl.BlockSpec(shape, lambda *_: (0,) * nd)


def _in_proj(x, lw, tabs, seq, *, ln=None, tm=512):
    t_tokens = x.shape[0]
    n_seq_tiles = seq // tm
    apply_ln = ln is not None
    row = lambda w: pl.BlockSpec((tm, w), lambda i: (i, 0))
    colt = lambda r: pl.BlockSpec((r, tm), lambda i: (0, i))
    tab = pl.BlockSpec((tm, D_HEAD_PAD), lambda i: (i % n_seq_tiles, 0))
    consts = [lw["wa"], lw["wt"], lw["gate_bias"], lw["q_norm_g"], lw["wq2"], lw["kv_norm_g"],
              lw["wuk"], lw["wuvt"]]
    in_specs = [row(D_MODEL)]
    args = [x]
    if apply_ln:
        in_specs += [_const_spec((1, D_MODEL))] * 2
        args += [ln[0], ln[1]]
    tab_t = pl.BlockSpec((D_HEAD_PAD, tm), lambda i: (0, i % n_seq_tiles))
    in_specs += [_const_spec(c.shape) for c in consts] + [tab_t, tab_t, tab, tab]
    args += consts + list(tabs)
    hw = N_HEADS * D_HEAD_PAD
    out_shape, out_specs = [], []
    if apply_ln:
        out_shape.append(jax.ShapeDtypeStruct((t_tokens, D_MODEL), F32))
        out_specs.append(row(D_MODEL))
    out_shape += [
        jax.ShapeDtypeStruct((hw, t_tokens), BF16),
        jax.ShapeDtypeStruct((t_tokens, hw), BF16),
        jax.ShapeDtypeStruct((N_HEADS * D_V, t_tokens), BF16),
        jax.ShapeDtypeStruct((t_tokens, POOL_WIDTH), F32),
        jax.ShapeDtypeStruct((t_tokens, ML_WIDTH), BF16),
        jax.ShapeDtypeStruct((t_tokens, ML_WIDTH), F32),
        jax.ShapeDtypeStruct((ML_WIDTH, t_tokens), BF16),
        jax.ShapeDtypeStruct((ML_VT_ROWS, t_tokens), BF16),
        jax.ShapeDtypeStruct((N_GATES, t_tokens), F32),
    ]
    out_specs += [colt(hw), row(hw), colt(N_HEADS * D_V), row(POOL_WIDTH), row(ML_WIDTH), row(ML_WIDTH),
                  colt(ML_WIDTH), colt(ML_VT_ROWS), colt(N_GATES)]
    return pl.pallas_call(
        functools.partial(_in_proj_kernel, apply_ln=apply_ln),
        grid=(t_tokens // tm,),
        in_specs=in_specs,
        out_specs=out_specs,
        out_shape=out_shape,
        compiler_params=pltpu.CompilerParams(dimension_semantics=("parallel",),
                                             vmem_limit_bytes=VMEM_LIMIT),
        name="in_proj_ln" if apply_ln else "in_proj",
    )(*args)


def _attention_kernel(q_ref, k_ref, vt_ref, o_ref, s_ref, smax_ref, acc_ref, m_ref, *, tk, sub, unroll):
    tq = q_ref.shape[1]
    n_sub = tq // sub
    n_k = k_ref.shape[0] // tk
    assert unroll % 2 == 0

    def scores(t, c):
        kblk = k_ref[pl.ds(pl.multiple_of(t * tk, tk), tk), :]
        return jnp.dot(kblk, q_ref[:, c * sub:(c + 1) * sub], preferred_element_type=F32)

    def step(t, parity, with_next):
        vblk = vt_ref[:, pl.ds(pl.multiple_of(t * tk, tk), tk)]
        vext = jnp.concatenate([vblk, jnp.ones((BF16_ROWS, tk), BF16)], axis=0)
        for c in range(n_sub):
            cols = slice(c * sub, (c + 1) * sub)
            if with_next:
                s_next = scores(t + 1, c)
                s_ref[1 - parity, c] = s_next
                smax_ref[1 - parity, :, cols] = jnp.max(s_next, axis=0, keepdims=True)
            s = s_ref[parity, c]
            m = m_ref[:, cols]
            m_new = jnp.maximum(m, smax_ref[parity, :, cols])
            alpha = jnp.exp2(m - m_new)
            p = jnp.exp2(s - m_new).astype(BF16)
            pv = jnp.dot(vext, p, preferred_element_type=F32)
            acc_ref[:, cols] = alpha * acc_ref[:, cols] + pv[:ACC_ROWS]
            m_ref[:, cols] = m_new

    def body(j, _):
        for i in range(unroll):
            step(unroll * j + i, i % 2, True)
        return 0

    m_ref[...] = jnp.full(m_ref.shape, NEG_BIG, F32)
    acc_ref[...] = jnp.zeros_like(acc_ref)
    for c in range(n_sub):
        s0 = scores(0, c)
        s_ref[0, c] = s0
        smax_ref[0, :, c * sub:(c + 1) * sub] = jnp.max(s0, axis=0, keepdims=True)
    trips = (n_k - 1) // unroll
    if trips:
        lax.fori_loop(0, trips, body, 0)
    for t in range(trips * unroll, n_k):
        step(t, t % 2, t + 1 < n_k)
    o_ref[...] = (acc_ref[:D_V, :] / acc_ref[D_V:D_V + 1, :]).astype(BF16)


def _attention(q, k, vt, batch, seq, *, tq=1024, tk=512, sub=256, unroll=6):
    n_q = seq // tq
    t_tokens = batch * seq
    k3 = k.reshape(batch, seq, N_HEADS * D_HEAD_PAD)
    return pl.pallas_call(
        functools.partial(_attention_kernel, tk=tk, sub=sub, unroll=unroll),
        grid=(batch, N_HEADS, n_q),
        in_specs=[
            pl.BlockSpec((D_HEAD_PAD, tq), lambda b, h, i: (h, b * n_q + i)),
            pl.BlockSpec((None, seq, D_HEAD_PAD), lambda b, h, i: (b, 0, h)),
            pl.BlockSpec((D_V, seq), lambda b, h, i: (h, b)),
        ],
        out_specs=pl.BlockSpec((D_V, tq), lambda b, h, i: (h, b * n_q + i)),
        out_shape=jax.ShapeDtypeStruct((N_HEADS * D_V, t_tokens), BF16),
        scratch_shapes=[pltpu.VMEM((2, tq // sub, tk, sub), F32), pltpu.VMEM((2, 1, tq), F32),
                        pltpu.VMEM((ACC_ROWS, tq), F32), pltpu.VMEM((1, tq), F32)],
        compiler_params=pltpu.CompilerParams(
            dimension_semantics=("parallel", "parallel", "arbitrary"), vmem_limit_bytes=VMEM_LIMIT),
        name="attention",
    )(q, k3, vt)


def _pool_kernel(x_ref, prev_ref, next_ref, w_ref, scale_ref, o_ref, *, seq):
    i = pl.program_id(1)
    ts = x_ref.shape[0]
    x = x_ref[...]
    prev = jnp.where(i == 0, 0.0, prev_ref[...])
    nxt = jnp.where(i == pl.num_programs(1) - 1, 0.0, next_ref[...])
    xe = jnp.concatenate([prev, x, nxt], axis=0)
    n = ts + 2 * POOL_HALO
    p2 = xe[0:n - 1] + xe[1:n]
    p4 = p2[0:n - 3] + p2[2:n - 1]
    p8 = p4[0:n - 7] + p4[4:n - 3]
    p16 = p8[0:n - 15] + p8[8:n - 7]
    sums = {2: p2, 4: p4, 8: p8, 16: p16}
    t = i * ts + lax.broadcasted_iota(jnp.int32, (ts, 1), 0)
    lane = lax.broadcasted_iota(jnp.int32, (1, POOL_WIDTH), 1)
    y = jnp.zeros((ts, POOL_WIDTH), F32)
    for g, w in enumerate(POOL_WINDOWS):
        start = POOL_HALO - w // 2
        win = sums[w][start:start + ts]
        cnt = (jnp.minimum(t + w // 2, seq) - jnp.maximum(t - w // 2, 0)).astype(F32)
        y = jnp.where(lane // POOL_GROUP == g, win / cnt - x, y)
    out = jnp.dot(y.astype(BF16), w_ref[...], preferred_element_type=F32) * scale_ref[...]
    o_ref[...] = out.astype(BF16)


def _pool(xp, w_bd, scale, batch, seq, *, ts=512):
    n_t = seq // ts
    hb = ts // POOL_HALO
    x3 = xp.reshape(batch, seq, POOL_WIDTH)
    out = pl.pallas_call(
        functools.partial(_pool_kernel, seq=seq),
        grid=(batch, n_t),
        in_specs=[
            pl.BlockSpec((None, ts, POOL_WIDTH), lambda b, i: (b, i, 0)),
            pl.BlockSpec((None, POOL_HALO, POOL_WIDTH), lambda b, i: (b, jnp.maximum(i * hb - 1, 0), 0)),
            pl.BlockSpec((None, POOL_HALO, POOL_WIDTH),
                         lambda b, i: (b, jnp.minimum((i + 1) * hb, seq // POOL_HALO - 1), 0)),
            _const_spec((POOL_WIDTH, POOL_WIDTH)),
            _const_spec((1, POOL_WIDTH)),
        ],
        out_specs=pl.BlockSpec((None, ts, POOL_WIDTH), lambda b, i: (b, i, 0)),
        out_shape=jax.ShapeDtypeStruct((batch, seq, POOL_WIDTH), BF16),
        compiler_params=pltpu.CompilerParams(dimension_semantics=("parallel", "parallel")),
        name="pool",
    )(x3, x3, x3, w_bd, scale)
    return out.reshape(batch * seq, POOL_WIDTH)


def _log_sigmoid(x):
    return -(jnp.maximum(-x, 0.0) + jnp.log1p(jnp.exp(-jnp.abs(x))))


_N_CHAINS = 2 * ML_HEADS
_AUX_KINDS = 5
_AUX_ROWS = _AUX_KINDS * _N_CHAINS


def _mlstm_intra_kernel(qt_ref, k_ref, vt_ref, gt_ref, af_ref, ab_ref, aux_ref, u_ref, *, chunks):
    L = ML_CHUNK
    r = lax.broadcasted_iota(jnp.int32, (L, L), 0)
    c = lax.broadcasted_iota(jnp.int32, (L, L), 1)
    lower, upper = c <= r, c >= r
    up_f = upper.astype(F32)
    lane_k = lax.broadcasted_iota(jnp.int32, (L, ML_WIDTH), 1)
    lane = lax.broadcasted_iota(jnp.int32, (ML_VT_HEAD, L), 1)
    slices = [slice(ci * L, (ci + 1) * L) for ci in range(chunks)]
    n_rows = chunks * _N_CHAINS
    gi = jnp.concatenate([gt_ref[:_N_CHAINS, sl] for sl in slices], axis=0)
    logf = _log_sigmoid(jnp.concatenate([gt_ref[_N_CHAINS:, sl] for sl in slices], axis=0))
    fwd_rows = lax.broadcasted_iota(jnp.int32, (n_rows, L), 0) % _N_CHAINS < ML_HEADS
    prefix = jnp.dot(logf, up_f, precision=lax.Precision.HIGHEST, preferred_element_type=F32)
    total = prefix[:, L - 1:L]
    b_all = jnp.where(fwd_rows, prefix, (total - prefix) + logf)
    bv = gi - b_all
    g = total + bv
    gmax = jnp.max(g, axis=1, keepdims=True)
    ws_all = jnp.exp(g - gmax)
    bv_cols = jnp.concatenate([bv, jnp.zeros((L - n_rows, L), F32)], axis=0).T
    total_b = jnp.broadcast_to(total, (n_rows, L))
    gmax_b = jnp.broadcast_to(gmax, (n_rows, L))
    raw_scores = []
    for ci, sl in enumerate(slices):
        rows = slice(ci * _N_CHAINS, (ci + 1) * _N_CHAINS)
        ws = ws_all[rows]
        aux_ref[0:_N_CHAINS, sl] = b_all[rows]
        aux_ref[3 * _N_CHAINS:4 * _N_CHAINS, sl] = total_b[rows]
        aux_ref[4 * _N_CHAINS:5 * _N_CHAINS, sl] = gmax_b[rows]
        k = k_ref[sl, :]
        u_out = [[None] * ML_HEADS for _ in range(2)]
        for h in range(ML_HEADS):
            vth = vt_ref[h * ML_VT_HEAD:(h + 1) * ML_VT_HEAD, sl].astype(F32)
            k_pair = k[:, (h // 2) * L:(h // 2 + 1) * L]
            for d in range(2):
                ch = d * ML_HEADS + h
                vw = (vth * ws[ch:ch + 1]).astype(BF16)
                u_out[d][h] = jnp.dot(vw, k_pair, preferred_element_type=F32)
        for d in range(2):
            for p in range(ML_HEADS // 2):
                u_ref[ci, d, p] = jnp.where(lane < ML_DIM, u_out[d][2 * p], u_out[d][2 * p + 1])
        qt = qt_ref[:, sl]
        for h in range(ML_HEADS):
            kh = jnp.where(lane_k // ML_DIM == h, k, jnp.zeros_like(k))
            raw_scores.append(jnp.dot(kh, qt, preferred_element_type=F32))
    for ci, sl in enumerate(slices):
        for h in range(ML_HEADS):
            sc_t = raw_scores[ci * ML_HEADS + h]
            vth = vt_ref[h * ML_VT_HEAD:(h + 1) * ML_VT_HEAD, sl]
            for d in range(2):
                ch = d * ML_HEADS + h
                row = ci * _N_CHAINS + ch
                valid = upper if d == 0 else lower
                dm_t = jnp.where(valid, b_all[row:row + 1] + bv_cols[:, row:row + 1], NEG_BIG)
                dmax = jnp.max(dm_t, axis=0, keepdims=True)
                sp_t = (sc_t * jnp.exp(dm_t - dmax)).astype(BF16)
                a_ext = jnp.dot(vth, sp_t, preferred_element_type=F32)
                a_ref = af_ref if d == 0 else ab_ref
                a_ref[h * ML_DIM:(h + 1) * ML_DIM, sl] = a_ext[:ML_DIM]
                aux_ref[_N_CHAINS + ch:_N_CHAINS + ch + 1, sl] = dmax
                aux_ref[2 * _N_CHAINS + ch:2 * _N_CHAINS + ch + 1, sl] = a_ext[ML_DIM:ML_DIM + 1]


def _mlstm_intra(qmt, km, vmt, gt, *, chunks=8):
    L = ML_CHUNK
    t_tokens = km.shape[0]
    w = chunks * L
    col = lambda rows: pl.BlockSpec((rows, w), lambda i: (0, i))
    return pl.pallas_call(
        functools.partial(_mlstm_intra_kernel, chunks=chunks),
        grid=(t_tokens // w,),
        in_specs=[col(ML_WIDTH), pl.BlockSpec((w, ML_WIDTH), lambda i: (i, 0)), col(ML_VT_ROWS), col(N_GATES)],
        out_specs=[col(ML_WIDTH), col(ML_WIDTH), col(_AUX_ROWS),
                   pl.BlockSpec((chunks, 2, ML_HEADS // 2, ML_VT_HEAD, L), lambda i: (i, 0, 0, 0, 0))],
        out_shape=[jax.ShapeDtypeStruct((ML_WIDTH, t_tokens), F32),
                   jax.ShapeDtypeStruct((ML_WIDTH, t_tokens), F32),
                   jax.ShapeDtypeStruct((_AUX_ROWS, t_tokens), F32),
                   jax.ShapeDtypeStruct((t_tokens // L, 2, ML_HEADS // 2, ML_VT_HEAD, L), F32)],
        compiler_params=pltpu.CompilerParams(dimension_semantics=("parallel",)),
        name="mlstm_intra",
    )(qmt, km, vmt, gt)


def _mlstm_scan_kernel(qtf_ref, qtb_ref, af_ref, ab_ref, auxf_ref, auxb_ref, uf_ref, ub_ref,
                       hf_ref, hb_ref, c_ref, m_ref, *, chunks):
    L = ML_CHUNK

    @pl.when(pl.program_id(1) == 0)
    def _():
        c_ref[...] = jnp.zeros_like(c_ref)
        m_ref[...] = jnp.zeros_like(m_ref)

    lane = lax.broadcasted_iota(jnp.int32, (ML_VT_HEAD, L), 1)
    lane_row = lax.broadcasted_iota(jnp.int32, (1, L), 1)
    fwd_rows = lax.broadcasted_iota(jnp.int32, (_N_CHAINS, L), 0) < ML_HEADS
    for i in range(chunks):
        pos = (i, chunks - 1 - i)
        sls = tuple(slice(q * L, (q + 1) * L) for q in pos)

        def aux(kind):
            rows = slice(kind * _N_CHAINS, (kind + 1) * _N_CHAINS)
            return jnp.where(fwd_rows, auxf_ref[rows, sls[0]], auxb_ref[rows, sls[1]])

        b, dmax, den_intra, b_last, gmax = (aux(kind) for kind in range(_AUX_KINDS))
        m_old = m_ref[...]
        m_inter = b + m_old
        m_j = jnp.maximum(m_inter, dmax)
        e_inter = jnp.exp(m_inter - m_j)
        e_intra = jnp.exp(dmax - m_j)
        e_floor = jnp.exp(-m_j)
        m_new = jnp.maximum(b_last + m_old, gmax)
        decay = jnp.exp(b_last + m_old - m_new)
        gain = jnp.exp(gmax - m_new)
        m_ref[...] = m_new
        dirs = ((qtf_ref, af_ref, uf_ref, hf_ref), (qtb_ref, ab_ref, ub_ref, hb_ref))
        for d, (qt_ref, a_ref, u_ref, h_ref) in enumerate(dirs):
            sl = sls[d]
            for p in range(ML_HEADS // 2):
                cp = c_ref[d, p]
                qtp = qt_ref[p * L:(p + 1) * L, sl]
                for hh in range(2):
                    h = 2 * p + hh
                    ch = d * ML_HEADS + h
                    own = lane < ML_DIM if hh == 0 else lane >= ML_DIM
                    cm = jnp.where(own, cp, 0.0).astype(BF16)
                    x_t = jnp.dot(cm, qtp, preferred_element_type=F32)
                    rows = slice(h * ML_DIM, (h + 1) * ML_DIM)
                    num = e_inter[ch:ch + 1] * x_t[:ML_DIM] + e_intra[ch:ch + 1] * a_ref[rows, sl]
                    den = (e_inter[ch:ch + 1] * x_t[ML_DIM:ML_DIM + 1]
                           + e_intra[ch:ch + 1] * den_intra[ch:ch + 1])
                    h_ref[rows, sl] = num / jnp.maximum(jnp.abs(den), e_floor[ch:ch + 1])
                lo, hi_ = d * ML_HEADS + 2 * p, d * ML_HEADS + 2 * p + 1
                decay_p = jnp.where(lane_row < ML_DIM, decay[lo:lo + 1], decay[hi_:hi_ + 1])
                gain_p = jnp.where(lane_row < ML_DIM, gain[lo:lo + 1], gain[hi_:hi_ + 1])
                c_ref[d, p] = decay_p * cp + gain_p * u_ref[pos[d], p]


def _mlstm_scan(qmt, a_f, a_b, aux, u, batch, seq, *, chunks=8):
    L = ML_CHUNK
    w = chunks * L
    nb = seq // w
    t_tokens = batch * seq
    fwd = lambda b, c: b * nb + c
    bwd = lambda b, c: b * nb + (nb - 1 - c)
    col = lambda f, rows: pl.BlockSpec((rows, w), lambda b, c: (0, f(b, c)))
    ublk = lambda f, d: pl.BlockSpec((chunks, None, ML_HEADS // 2, ML_VT_HEAD, L),
                                     lambda b, c: (f(b, c), d, 0, 0, 0))
    return pl.pallas_call(
        functools.partial(_mlstm_scan_kernel, chunks=chunks),
        grid=(batch, nb),
        in_specs=[col(fwd, ML_WIDTH), col(bwd, ML_WIDTH), col(fwd, ML_WIDTH), col(bwd, ML_WIDTH),
                  col(fwd, _AUX_ROWS), col(bwd, _AUX_ROWS), ublk(fwd, 0), ublk(bwd, 1)],
        out_specs=[col(fwd, ML_WIDTH), col(bwd, ML_WIDTH)],
        out_shape=[jax.ShapeDtypeStruct((ML_WIDTH, t_tokens), F32)] * 2,
        scratch_shapes=[pltpu.VMEM((2, ML_HEADS // 2, ML_VT_HEAD, L), F32), pltpu.VMEM((_N_CHAINS, L), F32)],
        compiler_params=pltpu.CompilerParams(dimension_semantics=("parallel", "arbitrary")),
        name="mlstm_scan",
    )(qmt, qmt, a_f, a_b, aux, aux, u, u)


def _out_ffn_kernel(x_ref, yat_ref, yp_ref, hf_ref, hb_ref, om_ref, ng_ref,
                    woa_ref, wop_ref, wom_ref, l1g_ref, l1b_ref, wg_ref, wu_ref, wd_ref,
                    l2g_ref, l2b_ref, o_ref):
    x = x_ref[...]
    parts = []
    for h in range(ML_HEADS):
        rows = slice(h * ML_DIM, (h + 1) * ML_DIM)
        ht = hf_ref[rows, :] + hb_ref[rows, :]
        hc = ht - jnp.mean(ht, axis=0, keepdims=True)
        var = jnp.mean(hc * hc, axis=0, keepdims=True)
        parts.append(hc * lax.rsqrt(var + LN_EPS))
    hn = jnp.concatenate(parts, axis=0).T * ng_ref[...]
    y_ml = (hn / (1.0 + jnp.exp(-om_ref[...]))).astype(BF16)
    y_at = yat_ref[...].T
    mix = (jnp.dot(y_at, woa_ref[...], preferred_element_type=F32)
           + jnp.dot(yp_ref[...], wop_ref[...], preferred_element_type=F32)
           + jnp.dot(y_ml, wom_ref[...], preferred_element_type=F32))
    x1 = _layer_norm(ALPHA * x + mix, l1g_ref[...], l1b_ref[...])
    x1b = x1.astype(BF16)
    gate = jnp.dot(x1b, wg_ref[...], preferred_element_type=F32)
    up = jnp.dot(x1b, wu_ref[...], preferred_element_type=F32)
    hid = (gate / (1.0 + jnp.exp(-gate)) * up).astype(BF16)
    ffn = jnp.dot(hid, wd_ref[...], preferred_element_type=F32)
    o_ref[...] = _layer_norm(ALPHA * x1 + ffn, l2g_ref[...], l2b_ref[...])


def _out_ffn(x, yat, yp, hf, hb, om, lw, *, tm=512):
    t_tokens = x.shape[0]
    row = lambda w: pl.BlockSpec((tm, w), lambda i: (i, 0))
    colt = lambda r: pl.BlockSpec((r, tm), lambda i: (0, i))
    single = lambda a: pl.BlockSpec(a.shape, lambda i: (0,) * a.ndim, pipeline_mode=pl.Buffered(1))
    consts = [lw["norm_g"], lw["wo_a"], lw["wo_p"], lw["wo_m"], lw["ln1_g"], lw["ln1_b"],
              lw["w_gate"], lw["w_up"], lw["w_down"], lw["ln2_g"], lw["ln2_b"]]
    return pl.pallas_call(
        _out_ffn_kernel,
        grid=(t_tokens // tm,),
        in_specs=[row(D_MODEL), colt(N_HEADS * D_V), row(POOL_WIDTH),
                  colt(ML_WIDTH), colt(ML_WIDTH), row(ML_WIDTH)] + [single(c) for c in consts],
        out_specs=row(D_MODEL),
        out_shape=jax.ShapeDtypeStruct((t_tokens, D_MODEL), F32),
        compiler_params=pltpu.CompilerParams(dimension_semantics=("parallel",),
                                             vmem_limit_bytes=VMEM_LIMIT),
        name="out_ffn",
    )(x, yat, yp, hf, hb, om, *consts)


def _head_pad_cols(w, n_heads, width, dst_lo):
    k = w.shape[0]
    w3 = w.reshape(k, n_heads, width)
    out = jnp.zeros((k, n_heads, D_HEAD_PAD), w.dtype)
    out = out.at[:, :, dst_lo:dst_lo + width].set(w3)
    return out.reshape(k, n_heads * D_HEAD_PAD)


def _rope_partner(w_rope):
    half = D_ROPE // 2
    return jnp.concatenate([-w_rope[..., half:], w_rope[..., :half]], axis=-1)


def _prep_layer_weights(l, w_in, q_norm_g, w_uq, kv_norm_g, w_ukv, w_pool, pool_scale, gate_bias,
                        norm_g, w_out, ln1_g, ln1_b, w_gate, w_up, w_down, ln2_g, ln2_b):
    wi = w_in[l]
    c = np.cumsum((0, 256, 128, 32, 256, 256, 256, 256, 256, 16))
    w_cq, w_ckv, w_kr, w_pl, w_qm, w_km, w_vm, w_om, w_g = (wi[:, c[j]:c[j + 1]] for j in range(9))
    kr_pad = jnp.zeros((D_MODEL, D_HEAD_PAD), F32).at[:, D_NOPE:D_NOPE + D_ROPE].set(w_kr)
    krs_pad = jnp.zeros((D_MODEL, D_HEAD_PAD), F32).at[:, D_NOPE:D_NOPE + D_ROPE].set(_rope_partner(w_kr))
    wa = jnp.concatenate([w_cq, w_ckv, kr_pad, krs_pad, w_pl, w_km * ML_DIM ** -0.5, w_om],
                         axis=1).astype(BF16)
    vm_t = jnp.zeros((ML_HEADS, ML_VT_HEAD, D_MODEL), F32).at[:, :ML_DIM, :].set(
        w_vm.T.reshape(ML_HEADS, ML_DIM, D_MODEL)).reshape(ML_VT_ROWS, D_MODEL)
    gate_order = np.array([0, 2, 1, 3])[:, None] * ML_HEADS + np.arange(ML_HEADS)[None, :]
    gate_order = gate_order.reshape(-1)
    wt = jnp.concatenate([w_qm.T, vm_t, w_g.T[gate_order]], axis=0).astype(BF16)
    uq = w_uq[l].reshape(Q_RANK, N_HEADS, D_NOPE + D_ROPE)
    uq_plain = jnp.zeros((Q_RANK, N_HEADS, D_HEAD_PAD), F32).at[:, :, :D_NOPE + D_ROPE].set(uq)
    uq_part = jnp.zeros((Q_RANK, N_HEADS, D_HEAD_PAD), F32).at[:, :, D_NOPE:D_NOPE + D_ROPE].set(
        _rope_partner(uq[:, :, D_NOPE:]))
    wq2 = jnp.concatenate([uq_plain.reshape(Q_RANK, -1), uq_part.reshape(Q_RANK, -1)], axis=1).T.astype(BF16)
    ukv = w_ukv[l].reshape(KV_RANK, N_HEADS, D_NOPE + D_V)
    wuk = _head_pad_cols(ukv[:, :, :D_NOPE].reshape(KV_RANK, -1), N_HEADS, D_NOPE, 0).astype(BF16)
    wuvt = ukv[:, :, D_NOPE:].reshape(KV_RANK, -1).T.astype(BF16)
    w_bd = jnp.zeros((POOL_WIDTH, POOL_WIDTH), F32)
    for g in range(len(POOL_WINDOWS)):
        s = slice(g * POOL_GROUP, (g + 1) * POOL_GROUP)
        w_bd = w_bd.at[s, s].set(w_pool[l, g])
    wo = w_out[l].astype(BF16)
    a_end = N_HEADS * D_V
    return {
        "wa": wa, "wt": wt, "gate_bias": gate_bias[l][gate_order].reshape(N_GATES, 1),
        "q_norm_g": q_norm_g[l].reshape(1, -1), "wq2": wq2, "kv_norm_g": kv_norm_g[l].reshape(1, -1),
        "wuk": wuk, "wuvt": wuvt, "w_pool": w_bd.astype(BF16), "pool_scale": pool_scale[l].reshape(1, -1),
        "norm_g": norm_g[l].reshape(1, -1),
        "wo_a": wo[:a_end], "wo_p": wo[a_end:a_end + POOL_WIDTH], "wo_m": wo[a_end + POOL_WIDTH:],
        "ln1_g": ln1_g[l].reshape(1, -1), "ln1_b": ln1_b[l].reshape(1, -1),
        "w_gate": w_gate[l].astype(BF16), "w_up": w_up[l].astype(BF16), "w_down": w_down[l].astype(BF16),
        "ln2_g": ln2_g[l].reshape(1, -1), "ln2_b": ln2_b[l].reshape(1, -1),
    }


def _rope_tables(seq):
    half = D_ROPE // 2
    inv = 1.0 / (ROPE_THETA ** (jnp.arange(0, D_ROPE, 2, dtype=F32) / D_ROPE))
    ang = jnp.arange(seq, dtype=F32)[:, None] * inv[None, :]
    cos, sin = jnp.cos(ang), jnp.sin(ang)
    one = jnp.ones((seq, D_NOPE), F32)
    zero_n = jnp.zeros((seq, D_NOPE), F32)
    zero_p = jnp.zeros((seq, D_HEAD_PAD - D_NOPE - D_ROPE), F32)
    q_scale = (D_NOPE + D_ROPE) ** -0.5 * LOG2E
    cq = jnp.concatenate([one, cos, cos, zero_p], axis=1) * q_scale
    sq = jnp.concatenate([zero_n, sin, sin, zero_p], axis=1) * q_scale
    ck = jnp.concatenate([zero_n, cos, cos, zero_p], axis=1)
    sk = jnp.concatenate([zero_n, sin, sin, zero_p], axis=1)
    return cq.T, sq.T, ck, sk


def _trunk(x, ln_in, layers):
    batch, seq, _ = x.shape
    tabs = _rope_tables(seq)
    xf = x.reshape(batch * seq, D_MODEL)
    for l, lw in enumerate(layers):
        outs = _in_proj(xf, lw, tabs, seq, ln=ln_in if l == 0 else None)
        if l == 0:
            xf, outs = outs[0], outs[1:]
        q, k, vt, xp, km, om, qmt, vmt, gt = outs
        yat = _attention(q, k, vt, batch, seq)
        yp = _pool(xp, lw["w_pool"], lw["pool_scale"], batch, seq)
        a_f, a_b, aux, u = _mlstm_intra(qmt, km, vmt, gt)
        hf, hb = _mlstm_scan(qmt, a_f, a_b, aux, u, batch, seq)
        xf = _out_ffn(xf, yat, yp, hf, hb, om, lw)
    return xf.reshape(batch, seq, D_MODEL)


def kernel(x_prompt, x_sample, ln_in_g, ln_in_b, w_in, q_norm_g, w_uq, kv_norm_g, w_ukv, w_pool, pool_scale,
           mlstm_gate_bias, mlstm_norm_g, w_out, ln1_g, ln1_b, w_gate, w_up, w_down, ln2_g, ln2_b):
    layers = [_prep_layer_weights(l, w_in, q_norm_g, w_uq, kv_norm_g, w_ukv, w_pool, pool_scale,
                                  mlstm_gate_bias, mlstm_norm_g, w_out, ln1_g, ln1_b, w_gate, w_up, w_down,
                                  ln2_g, ln2_b) for l in range(DEPTH)]
    ln_in = (ln_in_g.reshape(1, -1), ln_in_b.reshape(1, -1))
    return (_trunk(x_prompt, ln_in, layers), _trunk(x_sample, ln_in, layers))
```

```python
import functools
import math

import jax
import jax.numpy as jnp
import numpy as np
from jax import lax
from jax.experimental import pallas as pl
from jax.experimental.pallas import tpu as pltpu

F32 = jnp.float32
BF16 = jnp.bfloat16

D_MODEL = 1024
DEPTH = 4
N_HEADS = 8
D_NOPE = 64
D_ROPE = 32
D_V = 64
BF16_ROWS = 16
ACC_ROWS = 72
Q_RANK = 256
KV_RANK = 128
ROPE_THETA = 10000.0
D_HEAD_PAD = 128
POOL_WIDTH = 256
POOL_WINDOWS = (2, 4, 8, 16)
POOL_GROUP = 64
POOL_HALO = 8
ML_HEADS = 4
ML_DIM = 64
ML_WIDTH = 256
ML_CHUNK = 128
ML_VT_HEAD = 80
ML_VT_ROWS = ML_HEADS * ML_VT_HEAD
N_GATES = 16
NEG_BIG = -1e30
D_FF = 2816
ALPHA = (2 * DEPTH) ** 0.25
LN_EPS = 1e-5
LOG2E = 1.4426950408889634

VMEM_LIMIT = 56 * 1024 * 1024

_PA_CQ, _PA_CKV, _PA_KR, _PA_POOL, _PA_KM, _PA_OM, _PA_END = (0, 256, 384, 512, 768, 1024, 1280)

_NT = (((1,), (1,)), ((), ()))


def _layer_norm(x, g, b):
    mu = jnp.mean(x, axis=-1, keepdims=True)
    xc = x - mu
    var = jnp.mean(xc * xc, axis=-1, keepdims=True)
    return xc * lax.rsqrt(var + LN_EPS) * g + b


def _rms_norm(x, g):
    return x * lax.rsqrt(jnp.mean(x * x, axis=-1, keepdims=True) + LN_EPS) * g


def _in_proj_kernel(*refs, apply_ln):
    if apply_ln:
        (x_ref, lng_ref, lnb_ref, wa_ref, wt_ref, gb_ref, qg_ref, wq_ref, kvg_ref, wuk_ref, wuvt_ref,
         cq_ref, sq_ref, ck_ref, sk_ref,
         xn_ref, q_ref, k_ref, vt_ref, pool_ref, km_ref, om_ref, qmt_ref, vmt_ref, gt_ref) = refs
    else:
        (x_ref, wa_ref, wt_ref, gb_ref, qg_ref, wq_ref, kvg_ref, wuk_ref, wuvt_ref,
         cq_ref, sq_ref, ck_ref, sk_ref,
         q_ref, k_ref, vt_ref, pool_ref, km_ref, om_ref, qmt_ref, vmt_ref, gt_ref) = refs
    x = x_ref[...]
    if apply_ln:
        x = _layer_norm(x, lng_ref[...], lnb_ref[...])
        xn_ref[...] = x
    xb = x.astype(BF16)
    pa = jnp.dot(xb, wa_ref[...], preferred_element_type=F32)
    pt = lax.dot_general(wt_ref[...], xb, _NT, preferred_element_type=F32)
    qmt_ref[...] = pt[:ML_WIDTH].astype(BF16)
    vrow = lax.broadcasted_iota(jnp.int32, (ML_VT_ROWS, 1), 0) % ML_VT_HEAD
    vmt_ref[...] = (pt[ML_WIDTH:ML_WIDTH + ML_VT_ROWS] + (vrow == ML_DIM).astype(F32)).astype(BF16)
    gt_ref[...] = pt[ML_WIDTH + ML_VT_ROWS:] + gb_ref[...]

    pool_ref[...] = pa[:, _PA_POOL:_PA_KM]
    km_ref[...] = pa[:, _PA_KM:_PA_OM].astype(BF16)
    om_ref[...] = pa[:, _PA_OM:_PA_END]

    cqn = _rms_norm(pa[:, _PA_CQ:_PA_CKV], qg_ref[...]).astype(BF16)
    qt = lax.dot_general(wq_ref[...], cqn, _NT, preferred_element_type=F32)
    cq, sq = cq_ref[...], sq_ref[...]
    x1, x2, half = D_NOPE, D_NOPE + D_ROPE // 2, D_ROPE // 2
    for h in range(N_HEADS):
        lo = h * D_HEAD_PAD
        qh = qt[lo:lo + D_HEAD_PAD]
        partner = jnp.concatenate([qh[:x1], -qh[x2:x2 + half], qh[x1:x2], qh[x2 + half:]], axis=0)
        q_ref[lo:lo + D_HEAD_PAD, :] = (qh * cq + partner * sq).astype(BF16)

    ckvn = _rms_norm(pa[:, _PA_CKV:_PA_KR], kvg_ref[...]).astype(BF16)
    kn = jnp.dot(ckvn, wuk_ref[...], preferred_element_type=F32)
    kr_raw = pa[:, _PA_KR:_PA_POOL]
    lane = lax.broadcasted_iota(jnp.int32, (1, D_HEAD_PAD), 1)
    kr_partner = jnp.where(lane < x2, -pltpu.roll(kr_raw, D_HEAD_PAD - half, axis=1),
                           pltpu.roll(kr_raw, half, axis=1))
    kr = kr_raw * ck_ref[...] + kr_partner * sk_ref[...]
    for h in range(N_HEADS):
        lo = h * D_HEAD_PAD
        k_ref[:, lo:lo + D_HEAD_PAD] = (kn[:, lo:lo + D_HEAD_PAD] + kr).astype(BF16)
    vt = lax.dot_general(wuvt_ref[...], ckvn, _NT, preferred_element_type=F32)
    vt_ref[...] = vt.astype(BF16)


def _const_spec(shape):
    nd = len(shape)
    return pl.BlockSpec(shape, lambda *_: (0,) * nd)


def _in_proj(x, lw, tabs, seq, *, ln=None, tm=512):
    t_tokens = x.shape[0]
    n_seq_tiles = seq // tm
    apply_ln = ln is not None
    row = lambda w: pl.BlockSpec((tm, w), lambda i: (i, 0))
    colt = lambda r: pl.BlockSpec((r, tm), lambda i: (0, i))
    tab = pl.BlockSpec((tm, D_HEAD_PAD), lambda i: (i % n_seq_tiles, 0))
    consts = [lw["wa"], lw["wt"], lw["gate_bias"], lw["q_norm_g"], lw["wq2"], lw["kv_norm_g"],
              lw["wuk"], lw["wuvt"]]
    in_specs = [row(D_MODEL)]
    args = [x]
    if apply_ln:
        in_specs += [_const_spec((1, D_MODEL))] * 2
        args += [ln[0], ln[1]]
    tab_t = pl.BlockSpec((D_HEAD_PAD, tm), lambda i: (0, i % n_seq_tiles))
    in_specs += [_const_spec(c.shape) for c in consts] + [tab_t, tab_t, tab, tab]
    args += consts + list(tabs)
    hw = N_HEADS * D_HEAD_PAD
    out_shape, out_specs = [], []
    if apply_ln:
        out_shape.append(jax.ShapeDtypeStruct((t_tokens, D_MODEL), F32))
        out_specs.append(row(D_MODEL))
    out_shape += [
        jax.ShapeDtypeStruct((hw, t_tokens), BF16),
        jax.ShapeDtypeStruct((t_tokens, hw), BF16),
        jax.ShapeDtypeStruct((N_HEADS * D_V, t_tokens), BF16),
        jax.ShapeDtypeStruct((t_tokens, POOL_WIDTH), F32),
        jax.ShapeDtypeStruct((t_tokens, ML_WIDTH), BF16),
        jax.ShapeDtypeStruct((t_tokens, ML_WIDTH), F32),
        jax.ShapeDtypeStruct((ML_WIDTH, t_tokens), BF16),
        jax.ShapeDtypeStruct((ML_VT_ROWS, t_tokens), BF16),
        jax.ShapeDtypeStruct((N_GATES, t_tokens), F32),
    ]
    out_specs += [colt(hw), row(hw), colt(N_HEADS * D_V), row(POOL_WIDTH), row(ML_WIDTH), row(ML_WIDTH),
                  colt(ML_WIDTH), colt(ML_VT_ROWS), colt(N_GATES)]
    return pl.pallas_call(
        functools.partial(_in_proj_kernel, apply_ln=apply_ln),
        grid=(t_tokens // tm,),
        in_specs=in_specs,
        out_specs=out_specs,
        out_shape=out_shape,
        compiler_params=pltpu.CompilerParams(dimension_semantics=("parallel",),
                                             vmem_limit_bytes=VMEM_LIMIT),
        name="in_proj_ln" if apply_ln else "in_proj",
    )(*args)


def _attention_kernel(q_ref, k_ref, vt_ref, o_ref, s_ref, smax_ref, acc_ref, m_ref, *, tq, tk, sub, unroll):
    n_qt = q_ref.shape[1] // tq
    n_sub = tq // sub
    n_k = k_ref.shape[0] // tk
    assert unroll % 2 == 0 and n_k % 2 == 0

    def put_scores(qi, t, slot, c):
        lo = qi * tq + c * sub
        kblk = k_ref[pl.ds(pl.multiple_of(t * tk, tk), tk), :]
        s = jnp.dot(kblk, q_ref[:, lo:lo + sub], preferred_element_type=F32)
        s_ref[slot, c] = s
        smax_ref[slot, :, c * sub:(c + 1) * sub] = jnp.max(s, axis=0, keepdims=True)

    def step(qi, t, parity, nxt):
        vblk = vt_ref[:, pl.ds(pl.multiple_of(t * tk, tk), tk)]
        vext = jnp.concatenate([vblk, jnp.ones((BF16_ROWS, tk), BF16)], axis=0)
        for c in range(n_sub):
            if nxt is not None:
                put_scores(nxt[0], nxt[1], 1 - parity, c)
            cols = slice(qi * tq + c * sub, qi * tq + (c + 1) * sub)
            s = s_ref[parity, c]
            m = m_ref[:, cols]
            m_new = jnp.maximum(m, smax_ref[parity, :, c * sub:(c + 1) * sub])
            alpha = jnp.exp2(m - m_new)
            p = jnp.exp2(s - m_new).astype(BF16)
            pv = jnp.dot(vext, p, preferred_element_type=F32)
            acc_ref[:, cols] = alpha * acc_ref[:, cols] + pv[:ACC_ROWS]
            m_ref[:, cols] = m_new

    m_ref[...] = jnp.full(m_ref.shape, NEG_BIG, F32)
    acc_ref[...] = jnp.zeros_like(acc_ref)
    for c in range(n_sub):
        put_scores(0, 0, 0, c)
    trips = (n_k - 1) // unroll
    for qi in range(n_qt):
        def body(j, _, qi=qi):
            for i in range(unroll):
                t = unroll * j + i
                step(qi, t, i % 2, (qi, t + 1))
            return 0

        if trips:
            lax.fori_loop(0, trips, body, 0)
        for t in range(trips * unroll, n_k):
            nxt = (qi, t + 1) if t + 1 < n_k else ((qi + 1, 0) if qi + 1 < n_qt else None)
            step(qi, t, t % 2, nxt)
    o_ref[...] = (acc_ref[:D_V, :] / acc_ref[D_V:D_V + 1, :]).astype(BF16)


def _attention(q, k, vt, batch, seq, *, tq=1024, q_tiles=2, tk=512, sub=256, unroll=6):
    wq = tq * q_tiles
    assert seq % wq == 0 and seq % (2 * tk) == 0
    n_q = seq // wq
    t_tokens = batch * seq
    k3 = k.reshape(batch, seq, N_HEADS * D_HEAD_PAD)
    return pl.pallas_call(
        functools.partial(_attention_kernel, tq=tq, tk=tk, sub=sub, unroll=unroll),
        grid=(batch, N_HEADS, n_q),
        in_specs=[
            pl.BlockSpec((D_HEAD_PAD, wq), lambda b, h, i: (h, b * n_q + i)),
            pl.BlockSpec((None, seq, D_HEAD_PAD), lambda b, h, i: (b, 0, h)),
            pl.BlockSpec((D_V, seq), lambda b, h, i: (h, b)),
        ],
        out_specs=pl.BlockSpec((D_V, wq), lambda b, h, i: (h, b * n_q + i)),
        out_shape=jax.ShapeDtypeStruct((N_HEADS * D_V, t_tokens), BF16),
        scratch_shapes=[pltpu.VMEM((2, tq // sub, tk, sub), F32), pltpu.VMEM((2, 1, tq), F32),
                        pltpu.VMEM((ACC_ROWS, wq), F32), pltpu.VMEM((1, wq), F32)],
        compiler_params=pltpu.CompilerParams(
            dimension_semantics=("parallel", "parallel", "arbitrary"), vmem_limit_bytes=VMEM_LIMIT),
        name="attention",
    )(q, k3, vt)


def _pool_kernel(x_ref, prev_ref, next_ref, w_ref, scale_ref, o_ref, *, seq):
    i = pl.program_id(1)
    ts = x_ref.shape[0]
    x = x_ref[...]
    prev = jnp.where(i == 0, 0.0, prev_ref[...])
    nxt = jnp.where(i == pl.num_programs(1) - 1, 0.0, next_ref[...])
    xe = jnp.concatenate([prev, x, nxt], axis=0)
    n = ts + 2 * POOL_HALO
    half_w = POOL_WIDTH // 2
    lane = lax.broadcasted_iota(jnp.int32, (1, half_w), 1)
    t = i * ts + lax.broadcasted_iota(jnp.int32, (ts, 1), 0)
    halves = []
    for hf in range(2):
        xh = xe[:, hf * half_w:(hf + 1) * half_w]
        w_lo, w_hi = POOL_WINDOWS[2 * hf], POOL_WINDOWS[2 * hf + 1]
        p, w, length = xh, 1, n
        sums = {}
        while w < w_hi:
            length -= w
            p = p[0:length] + p[w:w + length]
            w *= 2
            sums[w] = p
        wins = []
        for wn in (w_lo, w_hi):
            start = POOL_HALO - wn // 2
            wins.append(sums[wn][start:start + ts])
        first = lane < POOL_GROUP
        win = jnp.where(first, wins[0], wins[1])
        hw = jnp.where(first, w_lo // 2, w_hi // 2)
        cnt = (jnp.minimum(t + hw, seq) - jnp.maximum(t - hw, 0)).astype(F32)
        halves.append(win / cnt - x[:, hf * half_w:(hf + 1) * half_w])
    y = jnp.concatenate(halves, axis=1)
    out = jnp.dot(y.astype(BF16), w_ref[...], preferred_element_type=F32) * scale_ref[...]
    o_ref[...] = out.astype(BF16)


def _pool(xp, w_bd, scale, batch, seq, *, ts=512):
    n_t = seq // ts
    hb = ts // POOL_HALO
    x3 = xp.reshape(batch, seq, POOL_WIDTH)
    out = pl.pallas_call(
        functools.partial(_pool_kernel, seq=seq),
        grid=(batch, n_t),
        in_specs=[
            pl.BlockSpec((None, ts, POOL_WIDTH), lambda b, i: (b, i, 0)),
            pl.BlockSpec((None, POOL_HALO, POOL_WIDTH), lambda b, i: (b, jnp.maximum(i * hb - 1, 0), 0)),
            pl.BlockSpec((None, POOL_HALO, POOL_WIDTH),
                         lambda b, i: (b, jnp.minimum((i + 1) * hb, seq // POOL_HALO - 1), 0)),
            _const_spec((POOL_WIDTH, POOL_WIDTH)),
            _const_spec((1, POOL_WIDTH)),
        ],
        out_specs=pl.BlockSpec((None, ts, POOL_WIDTH), lambda b, i: (b, i, 0)),
        out_shape=jax.ShapeDtypeStruct((batch, seq, POOL_WIDTH), BF16),
        compiler_params=pltpu.CompilerParams(dimension_semantics=("parallel", "parallel")),
        name="pool",
    )(x3, x3, x3, w_bd, scale)
    return out.reshape(batch * seq, POOL_WIDTH)


def _log_sigmoid(x):
    return -(jnp.maximum(-x, 0.0) + jnp.log1p(jnp.exp(-jnp.abs(x))))


_N_CHAINS = 2 * ML_HEADS
_AUX_KINDS = 5
_AUX_ROWS = _AUX_KINDS * _N_CHAINS


def _mlstm_intra_kernel(qt_ref, k_ref, vt_ref, gt_ref, af_ref, ab_ref, aux_ref, u_ref, *, chunks):
    L = ML_CHUNK
    r = lax.broadcasted_iota(jnp.int32, (L, L), 0)
    c = lax.broadcasted_iota(jnp.int32, (L, L), 1)
    lower, upper = c <= r, c >= r
    up_f = upper.astype(F32)
    lane_k = lax.broadcasted_iota(jnp.int32, (L, ML_WIDTH), 1)
    lane = lax.broadcasted_iota(jnp.int32, (ML_VT_HEAD, L), 1)
    slices = [slice(ci * L, (ci + 1) * L) for ci in range(chunks)]
    n_rows = chunks * _N_CHAINS
    gi = jnp.concatenate([gt_ref[:_N_CHAINS, sl] for sl in slices], axis=0)
    logf = _log_sigmoid(jnp.concatenate([gt_ref[_N_CHAINS:, sl] for sl in slices], axis=0))
    fwd_rows = lax.broadcasted_iota(jnp.int32, (n_rows, L), 0) % _N_CHAINS < ML_HEADS
    prefix = jnp.dot(logf, up_f, precision=lax.Precision.HIGHEST, preferred_element_type=F32)
    total = prefix[:, L - 1:L]
    b_all = jnp.where(fwd_rows, prefix, (total - prefix) + logf)
    bv = gi - b_all
    g = total + bv
    gmax = jnp.max(g, axis=1, keepdims=True)
    ws_all = jnp.exp(g - gmax)
    bv_cols = jnp.concatenate([bv, jnp.zeros((L - n_rows, L), F32)], axis=0).T
    total_b = jnp.broadcast_to(total, (n_rows, L))
    gmax_b = jnp.broadcast_to(gmax, (n_rows, L))
    raw_scores = []
    for ci, sl in enumerate(slices):
        rows = slice(ci * _N_CHAINS, (ci + 1) * _N_CHAINS)
        ws = ws_all[rows]
        aux_ref[0:_N_CHAINS, sl] = b_all[rows]
        aux_ref[3 * _N_CHAINS:4 * _N_CHAINS, sl] = total_b[rows]
        aux_ref[4 * _N_CHAINS:5 * _N_CHAINS, sl] = gmax_b[rows]
        k = k_ref[sl, :]
        u_out = [[None] * ML_HEADS for _ in range(2)]
        for h in range(ML_HEADS):
            vth = vt_ref[h * ML_VT_HEAD:(h + 1) * ML_VT_HEAD, sl].astype(F32)
            k_pair = k[:, (h // 2) * L:(h // 2 + 1) * L]
            for d in range(2):
                ch = d * ML_HEADS + h
                vw = (vth * ws[ch:ch + 1]).astype(BF16)
                u_out[d][h] = jnp.dot(vw, k_pair, preferred_element_type=F32)
        for d in range(2):
            for p in range(ML_HEADS // 2):
                u_ref[ci, d, p] = jnp.where(lane < ML_DIM, u_out[d][2 * p], u_out[d][2 * p + 1])
        qt = qt_ref[:, sl]
        for h in range(ML_HEADS):
            kh = jnp.where(lane_k // ML_DIM == h, k, jnp.zeros_like(k))
            raw_scores.append(jnp.dot(kh, qt, preferred_element_type=F32))
    for ci, sl in enumerate(slices):
        for h in range(ML_HEADS):
            sc_t = raw_scores[ci * ML_HEADS + h]
            vth = vt_ref[h * ML_VT_HEAD:(h + 1) * ML_VT_HEAD, sl]
            for d in range(2):
                ch = d * ML_HEADS + h
                row = ci * _N_CHAINS + ch
                valid = upper if d == 0 else lower
                dm_t = jnp.where(valid, b_all[row:row + 1] + bv_cols[:, row:row + 1], NEG_BIG)
                dmax = jnp.max(dm_t, axis=0, keepdims=True)
                sp_t = (sc_t * jnp.exp(dm_t - dmax)).astype(BF16)
                a_ext = jnp.dot(vth, sp_t, preferred_element_type=F32)
                a_ref = af_ref if d == 0 else ab_ref
                a_ref[h * ML_DIM:(h + 1) * ML_DIM, sl] = a_ext[:ML_DIM]
                aux_ref[_N_CHAINS + ch:_N_CHAINS + ch + 1, sl] = dmax
                aux_ref[2 * _N_CHAINS + ch:2 * _N_CHAINS + ch + 1, sl] = a_ext[ML_DIM:ML_DIM + 1]


def _mlstm_intra(qmt, km, vmt, gt, *, chunks=8):
    L = ML_CHUNK
    t_tokens = km.shape[0]
    w = chunks * L
    col = lambda rows: pl.BlockSpec((rows, w), lambda i: (0, i))
    return pl.pallas_call(
        functools.partial(_mlstm_intra_kernel, chunks=chunks),
        grid=(t_tokens // w,),
        in_specs=[col(ML_WIDTH), pl.BlockSpec((w, ML_WIDTH), lambda i: (i, 0)), col(ML_VT_ROWS), col(N_GATES)],
        out_specs=[col(ML_WIDTH), col(ML_WIDTH), col(_AUX_ROWS),
                   pl.BlockSpec((chunks, 2, ML_HEADS // 2, ML_VT_HEAD, L), lambda i: (i, 0, 0, 0, 0))],
        out_shape=[jax.ShapeDtypeStruct((ML_WIDTH, t_tokens), F32),
                   jax.ShapeDtypeStruct((ML_WIDTH, t_tokens), F32),
                   jax.ShapeDtypeStruct((_AUX_ROWS, t_tokens), F32),
                   jax.ShapeDtypeStruct((t_tokens // L, 2, ML_HEADS // 2, ML_VT_HEAD, L), F32)],
        compiler_params=pltpu.CompilerParams(dimension_semantics=("parallel",)),
        name="mlstm_intra",
    )(qmt, km, vmt, gt)


def _mlstm_scan_kernel(qtf_ref, qtb_ref, af_ref, ab_ref, auxf_ref, auxb_ref, uf_ref, ub_ref,
                       hf_ref, hb_ref, c_ref, m_ref, *, chunks):
    L = ML_CHUNK

    @pl.when(pl.program_id(1) == 0)
    def _():
        c_ref[...] = jnp.zeros_like(c_ref)
        m_ref[...] = jnp.zeros_like(m_ref)

    lane = lax.broadcasted_iota(jnp.int32, (ML_VT_HEAD, L), 1)
    lane_row = lax.broadcasted_iota(jnp.int32, (1, L), 1)
    fwd_rows = lax.broadcasted_iota(jnp.int32, (_N_CHAINS, L), 0) < ML_HEADS
    for i in range(chunks):
        pos = (i, chunks - 1 - i)
        sls = tuple(slice(q * L, (q + 1) * L) for q in pos)

        def aux(kind):
            rows = slice(kind * _N_CHAINS, (kind + 1) * _N_CHAINS)
            return jnp.where(fwd_rows, auxf_ref[rows, sls[0]], auxb_ref[rows, sls[1]])

        b, dmax, den_intra, b_last, gmax = (aux(kind) for kind in range(_AUX_KINDS))
        m_old = m_ref[...]
        m_inter = b + m_old
        m_j = jnp.maximum(m_inter, dmax)
        e_inter = jnp.exp(m_inter - m_j)
        e_intra = jnp.exp(dmax - m_j)
        e_floor = jnp.exp(-m_j)
        m_new = jnp.maximum(b_last + m_old, gmax)
        decay = jnp.exp(b_last + m_old - m_new)
        gain = jnp.exp(gmax - m_new)
        m_ref[...] = m_new
        dirs = ((qtf_ref, af_ref, uf_ref, hf_ref), (qtb_ref, ab_ref, ub_ref, hb_ref))
        for d, (qt_ref, a_ref, u_ref, h_ref) in enumerate(dirs):
            sl = sls[d]
            for p in range(ML_HEADS // 2):
                cp = c_ref[d, p]
                qtp = qt_ref[p * L:(p + 1) * L, sl]
                for hh in range(2):
                    h = 2 * p + hh
                    ch = d * ML_HEADS + h
                    own = lane < ML_DIM if hh == 0 else lane >= ML_DIM
                    cm = jnp.where(own, cp, 0.0).astype(BF16)
                    x_t = jnp.dot(cm, qtp, preferred_element_type=F32)
                    rows = slice(h * ML_DIM, (h + 1) * ML_DIM)
                    num = e_inter[ch:ch + 1] * x_t[:ML_DIM] + e_intra[ch:ch + 1] * a_ref[rows, sl]
                    den = (e_inter[ch:ch + 1] * x_t[ML_DIM:ML_DIM + 1]
                           + e_intra[ch:ch + 1] * den_intra[ch:ch + 1])
                    h_ref[rows, sl] = num / jnp.maximum(jnp.abs(den), e_floor[ch:ch + 1])
                lo, hi_ = d * ML_HEADS + 2 * p, d * ML_HEADS + 2 * p + 1
                decay_p = jnp.where(lane_row < ML_DIM, decay[lo:lo + 1], decay[hi_:hi_ + 1])
                gain_p = jnp.where(lane_row < ML_DIM, gain[lo:lo + 1], gain[hi_:hi_ + 1])
                c_ref[d, p] = decay_p * cp + gain_p * u_ref[pos[d], p]


def _mlstm_scan(qmt, a_f, a_b, aux, u, batch, seq, *, chunks=8):
    L = ML_CHUNK
    w = chunks * L
    nb = seq // w
    t_tokens = batch * seq
    fwd = lambda b, c: b * nb + c
    bwd = lambda b, c: b * nb + (nb - 1 - c)
    col = lambda f, rows: pl.BlockSpec((rows, w), lambda b, c: (0, f(b, c)))
    ublk = lambda f, d: pl.BlockSpec((chunks, None, ML_HEADS // 2, ML_VT_HEAD, L),
                                     lambda b, c: (f(b, c), d, 0, 0, 0))
    return pl.pallas_call(
        functools.partial(_mlstm_scan_kernel, chunks=chunks),
        grid=(batch, nb),
        in_specs=[col(fwd, ML_WIDTH), col(bwd, ML_WIDTH), col(fwd, ML_WIDTH), col(bwd, ML_WIDTH),
                  col(fwd, _AUX_ROWS), col(bwd, _AUX_ROWS), ublk(fwd, 0), ublk(bwd, 1)],
        out_specs=[col(fwd, ML_WIDTH), col(bwd, ML_WIDTH)],
        out_shape=[jax.ShapeDtypeStruct((ML_WIDTH, t_tokens), F32)] * 2,
        scratch_shapes=[pltpu.VMEM((2, ML_HEADS // 2, ML_VT_HEAD, L), F32), pltpu.VMEM((_N_CHAINS, L), F32)],
        compiler_params=pltpu.CompilerParams(dimension_semantics=("parallel", "arbitrary")),
        name="mlstm_scan",
    )(qmt, qmt, a_f, a_b, aux, aux, u, u)


def _out_ffn_kernel(x_ref, yat_ref, yp_ref, hf_ref, hb_ref, om_ref, ng_ref,
                    woa_ref, wop_ref, wom_ref, l1g_ref, l1b_ref, wg_ref, wu_ref, wd_ref,
                    l2g_ref, l2b_ref, o_ref):
    x = x_ref[...]
    parts = []
    for h in range(ML_HEADS):
        rows = slice(h * ML_DIM, (h + 1) * ML_DIM)
        ht = hf_ref[rows, :] + hb_ref[rows, :]
        hc = ht - jnp.mean(ht, axis=0, keepdims=True)
        var = jnp.mean(hc * hc, axis=0, keepdims=True)
        parts.append(hc * lax.rsqrt(var + LN_EPS))
    hn = jnp.concatenate(parts, axis=0).T * ng_ref[...]
    y_ml = (hn / (1.0 + jnp.exp(-om_ref[...]))).astype(BF16)
    y_at = yat_ref[...].T
    mix = (jnp.dot(y_at, woa_ref[...], preferred_element_type=F32)
           + jnp.dot(yp_ref[...], wop_ref[...], preferred_element_type=F32)
           + jnp.dot(y_ml, wom_ref[...], preferred_element_type=F32))
    x1 = _layer_norm(ALPHA * x + mix, l1g_ref[...], l1b_ref[...])
    x1b = x1.astype(BF16)
    gate = jnp.dot(x1b, wg_ref[...], preferred_element_type=F32)
    up = jnp.dot(x1b, wu_ref[...], preferred_element_type=F32)
    hid = (gate / (1.0 + jnp.exp(-gate)) * up).astype(BF16)
    ffn = jnp.dot(hid, wd_ref[...], preferred_element_type=F32)
    o_ref[...] = _layer_norm(ALPHA * x1 + ffn, l2g_ref[...], l2b_ref[...])


def _out_ffn(x, yat, yp, hf, hb, om, lw, *, tm=512):
    t_tokens = x.shape[0]
    row = lambda w: pl.BlockSpec((tm, w), lambda i: (i, 0))
    colt = lambda r: pl.BlockSpec((r, tm), lambda i: (0, i))
    single = lambda a: pl.BlockSpec(a.shape, lambda i: (0,) * a.ndim, pipeline_mode=pl.Buffered(1))
    consts = [lw["norm_g"], lw["wo_a"], lw["wo_p"], lw["wo_m"], lw["ln1_g"], lw["ln1_b"],
              lw["w_gate"], lw["w_up"], lw["w_down"], lw["ln2_g"], lw["ln2_b"]]
    return pl.pallas_call(
        _out_ffn_kernel,
        grid=(t_tokens // tm,),
        in_specs=[row(D_MODEL), colt(N_HEADS * D_V), row(POOL_WIDTH),
                  colt(ML_WIDTH), colt(ML_WIDTH), row(ML_WIDTH)] + [single(c) for c in consts],
        out_specs=row(D_MODEL),
        out_shape=jax.ShapeDtypeStruct((t_tokens, D_MODEL), F32),
        compiler_params=pltpu.CompilerParams(dimension_semantics=("parallel",),
                                             vmem_limit_bytes=VMEM_LIMIT),
        name="out_ffn",
    )(x, yat, yp, hf, hb, om, *consts)


def _head_pad_cols(w, n_heads, width, dst_lo):
    k = w.shape[0]
    w3 = w.reshape(k, n_heads, width)
    out = jnp.zeros((k, n_heads, D_HEAD_PAD), w.dtype)
    out = out.at[:, :, dst_lo:dst_lo + width].set(w3)
    return out.reshape(k, n_heads * D_HEAD_PAD)


def _prep_layer_weights(l, w_in, q_norm_g, w_uq, kv_norm_g, w_ukv, w_pool, pool_scale, gate_bias,
                        norm_g, w_out, ln1_g, ln1_b, w_gate, w_up, w_down, ln2_g, ln2_b):
    wi = w_in[l]
    c = np.cumsum((0, 256, 128, 32, 256, 256, 256, 256, 256, 16))
    w_cq, w_ckv, w_kr, w_pl, w_qm, w_km, w_vm, w_om, w_g = (wi[:, c[j]:c[j + 1]] for j in range(9))
    kr_pad = jnp.zeros((D_MODEL, D_HEAD_PAD), F32).at[:, D_NOPE:D_NOPE + D_ROPE].set(w_kr)
    wa = jnp.concatenate([w_cq, w_ckv, kr_pad, w_pl, w_km * ML_DIM ** -0.5, w_om],
                         axis=1).astype(BF16)
    vm_t = jnp.zeros((ML_HEADS, ML_VT_HEAD, D_MODEL), F32).at[:, :ML_DIM, :].set(
        w_vm.T.reshape(ML_HEADS, ML_DIM, D_MODEL)).reshape(ML_VT_ROWS, D_MODEL)
    gate_order = np.array([0, 2, 1, 3])[:, None] * ML_HEADS + np.arange(ML_HEADS)[None, :]
    gate_order = gate_order.reshape(-1)
    wt = jnp.concatenate([w_qm.T, vm_t, w_g.T[gate_order]], axis=0).astype(BF16)
    uq = w_uq[l].reshape(Q_RANK, N_HEADS, D_NOPE + D_ROPE)
    uq_plain = jnp.zeros((Q_RANK, N_HEADS, D_HEAD_PAD), F32).at[:, :, :D_NOPE + D_ROPE].set(uq)
    wq2 = uq_plain.reshape(Q_RANK, -1).T.astype(BF16)
    ukv = w_ukv[l].reshape(KV_RANK, N_HEADS, D_NOPE + D_V)
    wuk = _head_pad_cols(ukv[:, :, :D_NOPE].reshape(KV_RANK, -1), N_HEADS, D_NOPE, 0).astype(BF16)
    wuvt = ukv[:, :, D_NOPE:].reshape(KV_RANK, -1).T.astype(BF16)
    w_bd = jnp.zeros((POOL_WIDTH, POOL_WIDTH), F32)
    for g in range(len(POOL_WINDOWS)):
        s = slice(g * POOL_GROUP, (g + 1) * POOL_GROUP)
        w_bd = w_bd.at[s, s].set(w_pool[l, g])
    wo = w_out[l].astype(BF16)
    a_end = N_HEADS * D_V
    return {
        "wa": wa, "wt": wt, "gate_bias": gate_bias[l][gate_order].reshape(N_GATES, 1),
        "q_norm_g": q_norm_g[l].reshape(1, -1), "wq2": wq2, "kv_norm_g": kv_norm_g[l].reshape(1, -1),
        "wuk": wuk, "wuvt": wuvt, "w_pool": w_bd.astype(BF16), "pool_scale": pool_scale[l].reshape(1, -1),
        "norm_g": norm_g[l].reshape(1, -1),
        "wo_a": wo[:a_end], "wo_p": wo[a_end:a_end + POOL_WIDTH], "wo_m": wo[a_end + POOL_WIDTH:],
        "ln1_g": ln1_g[l].reshape(1, -1), "ln1_b": ln1_b[l].reshape(1, -1),
        "w_gate": w_gate[l].astype(BF16), "w_up": w_up[l].astype(BF16), "w_down": w_down[l].astype(BF16),
        "ln2_g": ln2_g[l].reshape(1, -1), "ln2_b": ln2_b[l].reshape(1, -1),
    }


def _rope_tables(seq):
    half = D_ROPE // 2
    inv = 1.0 / (ROPE_THETA ** (jnp.arange(0, D_ROPE, 2, dtype=F32) / D_ROPE))
    ang = jnp.arange(seq, dtype=F32)[:, None] * inv[None, :]
    cos, sin = jnp.cos(ang), jnp.sin(ang)
    one = jnp.ones((seq, D_NOPE), F32)
    zero_n = jnp.zeros((seq, D_NOPE), F32)
    zero_p = jnp.zeros((seq, D_HEAD_PAD - D_NOPE - D_ROPE), F32)
    q_scale = (D_NOPE + D_ROPE) ** -0.5 * LOG2E
    cq = jnp.concatenate([one, cos, cos, zero_p], axis=1) * q_scale
    sq = jnp.concatenate([zero_n, sin, sin, zero_p], axis=1) * q_scale
    ck = jnp.concatenate([zero_n, cos, cos, zero_p], axis=1)
    sk = jnp.concatenate([zero_n, sin, sin, zero_p], axis=1)
    return cq.T, sq.T, ck, sk


def _trunk(x, ln_in, layers):
    batch, seq, _ = x.shape
    tabs = _rope_tables(seq)
    xf = x.reshape(batch * seq, D_MODEL)
    for l, lw in enumerate(layers):
        outs = _in_proj(xf, lw, tabs, seq, ln=ln_in if l == 0 else None)
        if l == 0:
            xf, outs = outs[0], outs[1:]
        q, k, vt, xp, km, om, qmt, vmt, gt = outs
        yat = _attention(q, k, vt, batch, seq)
        yp = _pool(xp, lw["w_pool"], lw["pool_scale"], batch, seq)
        a_f, a_b, aux, u = _mlstm_intra(qmt, km, vmt, gt)
        hf, hb = _mlstm_scan(qmt, a_f, a_b, aux, u, batch, seq)
        xf = _out_ffn(xf, yat, yp, hf, hb, om, lw)
    return xf.reshape(batch, seq, D_MODEL)


def kernel(x_prompt, x_sample, ln_in_g, ln_in_b, w_in, q_norm_g, w_uq, kv_norm_g, w_ukv, w_pool, pool_scale,
           mlstm_gate_bias, mlstm_norm_g, w_out, ln1_g, ln1_b, w_gate, w_up, w_down, ln2_g, ln2_b):
    layers = [_prep_layer_weights(l, w_in, q_norm_g, w_uq, kv_norm_g, w_ukv, w_pool, pool_scale,
                                  mlstm_gate_bias, mlstm_norm_g, w_out, ln1_g, ln1_b, w_gate, w_up, w_down,
                                  ln2_g, ln2_b) for l in range(DEPTH)]
    ln_in = (ln_in_g.reshape(1, -1), ln_in_b.reshape(1, -1))
    return (_trunk(x_prompt, ln_in, layers), _trunk(x_sample, ln_in, layers))
```

```python
import functools
import math

import jax
import jax.numpy as jnp
import numpy as np
from jax import lax
from jax.experimental import pallas as pl
from jax.experimental.pallas import tpu as pltpu

F32 = jnp.float32
BF16 = jnp.bfloat16

D_MODEL = 1024
DEPTH = 4
N_HEADS = 8
D_NOPE = 64
D_ROPE = 32
D_V = 64
BF16_ROWS = 16
ACC_ROWS = 72
Q_RANK = 256
KV_RANK = 128
ROPE_THETA = 10000.0
D_HEAD_PAD = 128
POOL_WIDTH = 256
POOL_WINDOWS = (2, 4, 8, 16)
POOL_GROUP = 64
POOL_HALO = 8
ML_HEADS = 4
ML_DIM = 64
ML_WIDTH = 256
ML_CHUNK = 128
ML_VT_HEAD = 80
ML_VT_ROWS = ML_HEADS * ML_VT_HEAD
N_GATES = 16
NEG_BIG = -1e30
D_FF = 2816
ALPHA = (2 * DEPTH) ** 0.25
LN_EPS = 1e-5
LOG2E = 1.4426950408889634

VMEM_LIMIT = 56 * 1024 * 1024

_PA_CQ, _PA_CKV, _PA_KR, _PA_POOL, _PA_KM, _PA_OM, _PA_END = (0, 256, 384, 512, 768, 1024, 1280)

_NT = (((1,), (1,)), ((), ()))


def _layer_norm(x, g, b):
    mu = jnp.mean(x, axis=-1, keepdims=True)
    xc = x - mu
    var = jnp.mean(xc * xc, axis=-1, keepdims=True)
    return xc * lax.rsqrt(var + LN_EPS) * g + b


def _rms_norm(x, g):
    return x * lax.rsqrt(jnp.mean(x * x, axis=-1, keepdims=True) + LN_EPS) * g


def _in_proj_kernel(*refs, apply_ln):
    if apply_ln:
        (x_ref, lng_ref, lnb_ref, wa_ref, wt_ref, gb_ref, qg_ref, wq_ref, kvg_ref, wuk_ref, wuvt_ref,
         cq_ref, sq_ref, ck_ref, sk_ref,
         xn_ref, q_ref, k_ref, vt_ref, pool_ref, km_ref, om_ref, qmt_ref, vmt_ref, gt_ref) = refs
    else:
        (x_ref, wa_ref, wt_ref, gb_ref, qg_ref, wq_ref, kvg_ref, wuk_ref, wuvt_ref,
         cq_ref, sq_ref, ck_ref, sk_ref,
         q_ref, k_ref, vt_ref, pool_ref, km_ref, om_ref, qmt_ref, vmt_ref, gt_ref) = refs
    x = x_ref[...]
    if apply_ln:
        x = _layer_norm(x, lng_ref[...], lnb_ref[...])
        xn_ref[...] = x
    xb = x.astype(BF16)
    pa = jnp.dot(xb, wa_ref[...], preferred_element_type=F32)
    pt = lax.dot_general(wt_ref[...], xb, _NT, preferred_element_type=F32)
    qmt_ref[...] = pt[:ML_WIDTH].astype(BF16)
    vrow = lax.broadcasted_iota(jnp.int32, (ML_VT_ROWS, 1), 0) % ML_VT_HEAD
    vmt_ref[...] = (pt[ML_WIDTH:ML_WIDTH + ML_VT_ROWS] + (vrow == ML_DIM).astype(F32)).astype(BF16)
    gt_ref[...] = pt[ML_WIDTH + ML_VT_ROWS:] + gb_ref[...]

    pool_ref[...] = pa[:, _PA_POOL:_PA_KM]
    km_ref[...] = pa[:, _PA_KM:_PA_OM].astype(BF16)
    om_ref[...] = pa[:, _PA_OM:_PA_END]

    cqn = _rms_norm(pa[:, _PA_CQ:_PA_CKV], qg_ref[...]).astype(BF16)
    qt = lax.dot_general(wq_ref[...], cqn, _NT, preferred_element_type=F32)
    cq, sq = cq_ref[...], sq_ref[...]
    x1, x2, half = D_NOPE, D_NOPE + D_ROPE // 2, D_ROPE // 2
    for h in range(N_HEADS):
        lo = h * D_HEAD_PAD
        qh = qt[lo:lo + D_HEAD_PAD]
        partner = jnp.concatenate([qh[:x1], -qh[x2:x2 + half], qh[x1:x2], qh[x2 + half:]], axis=0)
        q_ref[lo:lo + D_HEAD_PAD, :] = (qh * cq + partner * sq).astype(BF16)

    ckvn = _rms_norm(pa[:, _PA_CKV:_PA_KR], kvg_ref[...]).astype(BF16)
    kn = jnp.dot(ckvn, wuk_ref[...], preferred_element_type=F32)
    kr_raw = pa[:, _PA_KR:_PA_POOL]
    lane = lax.broadcasted_iota(jnp.int32, (1, D_HEAD_PAD), 1)
    kr_partner = jnp.where(lane < x2, -pltpu.roll(kr_raw, D_HEAD_PAD - half, axis=1),
                           pltpu.roll(kr_raw, half, axis=1))
    kr = kr_raw * ck_ref[...] + kr_partner * sk_ref[...]
    for h in range(N_HEADS):
        lo = h * D_HEAD_PAD
        k_ref[:, lo:lo + D_HEAD_PAD] = (kn[:, lo:lo + D_HEAD_PAD] + kr).astype(BF16)
    vt = lax.dot_general(wuvt_ref[...], ckvn, _NT, preferred_element_type=F32)
    vt_ref[...] = vt.astype(BF16)


def _const_spec(shape):
    nd = len(shape)
    return pl.BlockSpec(shape, lambda *_: (0,) * nd)


def _in_proj(x, lw, tabs, seq, *, ln=None, tm=512):
    t_tokens = x.shape[0]
    n_seq_tiles = seq // tm
    apply_ln = ln is not None
    row = lambda w: pl.BlockSpec((tm, w), lambda i: (i, 0))
    colt = lambda r: pl.BlockSpec((r, tm), lambda i: (0, i))
    tab = pl.BlockSpec((tm, D_HEAD_PAD), lambda i: (i % n_seq_tiles, 0))
    consts = [lw["wa"], lw["wt"], lw["gate_bias"], lw["q_norm_g"], lw["wq2"], lw["kv_norm_g"],
              lw["wuk"], lw["wuvt"]]
    in_specs = [row(D_MODEL)]
    args = [x]
    if apply_ln:
        in_specs += [_const_spec((1, D_MODEL))] * 2
        args += [ln[0], ln[1]]
    tab_t = pl.BlockSpec((D_HEAD_PAD, tm), lambda i: (0, i % n_seq_tiles))
    in_specs += [_const_spec(c.shape) for c in consts] + [tab_t, tab_t, tab, tab]
    args += consts + list(tabs)
    hw = N_HEADS * D_HEAD_PAD
    out_shape, out_specs = [], []
    if apply_ln:
        out_shape.append(jax.ShapeDtypeStruct((t_tokens, D_MODEL), F32))
        out_specs.append(row(D_MODEL))
    out_shape += [
        jax.ShapeDtypeStruct((hw, t_tokens), BF16),
        jax.ShapeDtypeStruct((t_tokens, hw), BF16),
        jax.ShapeDtypeStruct((N_HEADS * D_V, t_tokens), BF16),
        jax.ShapeDtypeStruct((t_tokens, POOL_WIDTH), F32),
        jax.ShapeDtypeStruct((t_tokens, ML_WIDTH), BF16),
        jax.ShapeDtypeStruct((t_tokens, ML_WIDTH), F32),
        jax.ShapeDtypeStruct((ML_WIDTH, t_tokens), BF16),
        jax.ShapeDtypeStruct((ML_VT_ROWS, t_tokens), BF16),
        jax.ShapeDtypeStruct((N_GATES, t_tokens), F32),
    ]
    out_specs += [colt(hw), row(hw), colt(N_HEADS * D_V), row(POOL_WIDTH), row(ML_WIDTH), row(ML_WIDTH),
                  colt(ML_WIDTH), colt(ML_VT_ROWS), colt(N_GATES)]
    return pl.pallas_call(
        functools.partial(_in_proj_kernel, apply_ln=apply_ln),
        grid=(t_tokens // tm,),
        in_specs=in_specs,
        out_specs=out_specs,
        out_shape=out_shape,
        compiler_params=pltpu.CompilerParams(dimension_semantics=("parallel",),
                                             vmem_limit_bytes=VMEM_LIMIT),
        name="in_proj_ln" if apply_ln else "in_proj",
    )(*args)


def _attention_kernel(q_ref, k_ref, vt_ref, o_ref, s_ref, smax_ref, acc_ref, m_ref, *, tq, tk, sub, unroll):
    n_qt = q_ref.shape[1] // tq
    n_sub = tq // sub
    n_k = k_ref.shape[0] // tk
    assert unroll % 2 == 0 and n_k % 2 == 0

    def put_scores(qi, t, slot, c):
        lo = qi * tq + c * sub
        kblk = k_ref[pl.ds(pl.multiple_of(t * tk, tk), tk), :]
        s = jnp.dot(kblk, q_ref[:, lo:lo + sub], preferred_element_type=F32)
        s_ref[slot, c] = s
        smax_ref[slot, :, c * sub:(c + 1) * sub] = jnp.max(s, axis=0, keepdims=True)

    def step(qi, t, parity, nxt):
        vblk = vt_ref[:, pl.ds(pl.multiple_of(t * tk, tk), tk)]
        vext = jnp.concatenate([vblk, jnp.ones((BF16_ROWS, tk), BF16)], axis=0)
        for c in range(n_sub):
            if nxt is not None:
                put_scores(nxt[0], nxt[1], 1 - parity, c)
            cols = slice(qi * tq + c * sub, qi * tq + (c + 1) * sub)
            s = s_ref[parity, c]
            m = m_ref[:, cols]
            m_new = jnp.maximum(m, smax_ref[parity, :, c * sub:(c + 1) * sub])
            alpha = jnp.exp2(m - m_new)
            p = jnp.exp2(s - m_new).astype(BF16)
            pv = jnp.dot(vext, p, preferred_element_type=F32)
            acc_ref[:, cols] = alpha * acc_ref[:, cols] + pv[:ACC_ROWS]
            m_ref[:, cols] = m_new

    m_ref[...] = jnp.full(m_ref.shape, NEG_BIG, F32)
    acc_ref[...] = jnp.zeros_like(acc_ref)
    for c in range(n_sub):
        put_scores(0, 0, 0, c)
    trips = (n_k - 1) // unroll
    for qi in range(n_qt):
        def body(j, _, qi=qi):
            for i in range(unroll):
                t = unroll * j + i
                step(qi, t, i % 2, (qi, t + 1))
            return 0

        if trips:
            lax.fori_loop(0, trips, body, 0)
        for t in range(trips * unroll, n_k):
            nxt = (qi, t + 1) if t + 1 < n_k else ((qi + 1, 0) if qi + 1 < n_qt else None)
            step(qi, t, t % 2, nxt)
    o_ref[...] = (acc_ref[:D_V, :] / acc_ref[D_V:D_V + 1, :]).astype(BF16)


def _attention(q, k, vt, batch, seq, *, tq=1024, q_tiles=2, tk=512, sub=256, unroll=10):
    wq = tq * q_tiles
    assert seq % wq == 0 and seq % (2 * tk) == 0
    n_q = seq // wq
    t_tokens = batch * seq
    k3 = k.reshape(batch, seq, N_HEADS * D_HEAD_PAD)
    return pl.pallas_call(
        functools.partial(_attention_kernel, tq=tq, tk=tk, sub=sub, unroll=unroll),
        grid=(batch, N_HEADS, n_q),
        in_specs=[
            pl.BlockSpec((D_HEAD_PAD, wq), lambda b, h, i: (h, b * n_q + i)),
            pl.BlockSpec((None, seq, D_HEAD_PAD), lambda b, h, i: (b, 0, h)),
            pl.BlockSpec((D_V, seq), lambda b, h, i: (h, b)),
        ],
        out_specs=pl.BlockSpec((D_V, wq), lambda b, h, i: (h, b * n_q + i)),
        out_shape=jax.ShapeDtypeStruct((N_HEADS * D_V, t_tokens), BF16),
        scratch_shapes=[pltpu.VMEM((2, tq // sub, tk, sub), F32), pltpu.VMEM((2, 1, tq), F32),
                        pltpu.VMEM((ACC_ROWS, wq), F32), pltpu.VMEM((1, wq), F32)],
        compiler_params=pltpu.CompilerParams(
            dimension_semantics=("parallel", "parallel", "arbitrary"), vmem_limit_bytes=VMEM_LIMIT),
        name="attention",
    )(q, k3, vt)


def _pool_kernel(x_ref, prev_ref, next_ref, w_ref, scale_ref, o_ref, *, seq):
    i = pl.program_id(1)
    ts = x_ref.shape[0]
    x = x_ref[...]
    prev = jnp.where(i == 0, 0.0, prev_ref[...])
    nxt = jnp.where(i == pl.num_programs(1) - 1, 0.0, next_ref[...])
    xe = jnp.concatenate([prev, x, nxt], axis=0)
    n = ts + 2 * POOL_HALO
    half_w = POOL_WIDTH // 2
    lane = lax.broadcasted_iota(jnp.int32, (1, half_w), 1)
    t = i * ts + lax.broadcasted_iota(jnp.int32, (ts, 1), 0)
    halves = []
    for hf in range(2):
        xh = xe[:, hf * half_w:(hf + 1) * half_w]
        w_lo, w_hi = POOL_WINDOWS[2 * hf], POOL_WINDOWS[2 * hf + 1]
        p, w, length = xh, 1, n
        sums = {}
        while w < w_hi:
            length -= w
            p = p[0:length] + p[w:w + length]
            w *= 2
            sums[w] = p
        wins = []
        for wn in (w_lo, w_hi):
            start = POOL_HALO - wn // 2
            wins.append(sums[wn][start:start + ts])
        first = lane < POOL_GROUP
        win = jnp.where(first, wins[0], wins[1])
        hw = jnp.where(first, w_lo // 2, w_hi // 2)
        cnt = (jnp.minimum(t + hw, seq) - jnp.maximum(t - hw, 0)).astype(F32)
        halves.append(win / cnt - x[:, hf * half_w:(hf + 1) * half_w])
    y = jnp.concatenate(halves, axis=1)
    out = jnp.dot(y.astype(BF16), w_ref[...], preferred_element_type=F32) * scale_ref[...]
    o_ref[...] = out.astype(BF16)


def _pool(xp, w_bd, scale, batch, seq, *, ts=512):
    n_t = seq // ts
    hb = ts // POOL_HALO
    x3 = xp.reshape(batch, seq, POOL_WIDTH)
    out = pl.pallas_call(
        functools.partial(_pool_kernel, seq=seq),
        grid=(batch, n_t),
        in_specs=[
            pl.BlockSpec((None, ts, POOL_WIDTH), lambda b, i: (b, i, 0)),
            pl.BlockSpec((None, POOL_HALO, POOL_WIDTH), lambda b, i: (b, jnp.maximum(i * hb - 1, 0), 0)),
            pl.BlockSpec((None, POOL_HALO, POOL_WIDTH),
                         lambda b, i: (b, jnp.minimum((i + 1) * hb, seq // POOL_HALO - 1), 0)),
            _const_spec((POOL_WIDTH, POOL_WIDTH)),
            _const_spec((1, POOL_WIDTH)),
        ],
        out_specs=pl.BlockSpec((None, ts, POOL_WIDTH), lambda b, i: (b, i, 0)),
        out_shape=jax.ShapeDtypeStruct((batch, seq, POOL_WIDTH), BF16),
        compiler_params=pltpu.CompilerParams(dimension_semantics=("parallel", "parallel")),
        name="pool",
    )(x3, x3, x3, w_bd, scale)
    return out.reshape(batch * seq, POOL_WIDTH)


def _log_sigmoid(x):
    return -(jnp.maximum(-x, 0.0) + jnp.log1p(jnp.exp(-jnp.abs(x))))


_N_CHAINS = 2 * ML_HEADS
_AUX_KINDS = 5
_AUX_ROWS = _AUX_KINDS * _N_CHAINS


def _mlstm_intra_kernel(qt_ref, k_ref, vt_ref, gt_ref, af_ref, ab_ref, aux_ref, u_ref, *, chunks):
    L = ML_CHUNK
    r = lax.broadcasted_iota(jnp.int32, (L, L), 0)
    c = lax.broadcasted_iota(jnp.int32, (L, L), 1)
    lower, upper = c <= r, c >= r
    up_f = upper.astype(F32)
    lane_k = lax.broadcasted_iota(jnp.int32, (L, ML_WIDTH), 1)
    lane = lax.broadcasted_iota(jnp.int32, (ML_VT_HEAD, L), 1)
    slices = [slice(ci * L, (ci + 1) * L) for ci in range(chunks)]
    n_rows = chunks * _N_CHAINS
    gi = jnp.concatenate([gt_ref[:_N_CHAINS, sl] for sl in slices], axis=0)
    logf = _log_sigmoid(jnp.concatenate([gt_ref[_N_CHAINS:, sl] for sl in slices], axis=0))
    fwd_rows = lax.broadcasted_iota(jnp.int32, (n_rows, L), 0) % _N_CHAINS < ML_HEADS
    prefix = jnp.dot(logf, up_f, precision=lax.Precision.HIGHEST, preferred_element_type=F32)
    total = prefix[:, L - 1:L]
    b_all = jnp.where(fwd_rows, prefix, (total - prefix) + logf)
    bv = gi - b_all
    g = total + bv
    gmax = jnp.max(g, axis=1, keepdims=True)
    ws_all = jnp.exp(g - gmax)
    bv_cols = jnp.concatenate([bv, jnp.zeros((L - n_rows, L), F32)], axis=0).T
    total_b = jnp.broadcast_to(total, (n_rows, L))
    gmax_b = jnp.broadcast_to(gmax, (n_rows, L))
    raw_scores = []
    for ci, sl in enumerate(slices):
        rows = slice(ci * _N_CHAINS, (ci + 1) * _N_CHAINS)
        ws = ws_all[rows]
        aux_ref[0:_N_CHAINS, sl] = b_all[rows]
        aux_ref[3 * _N_CHAINS:4 * _N_CHAINS, sl] = total_b[rows]
        aux_ref[4 * _N_CHAINS:5 * _N_CHAINS, sl] = gmax_b[rows]
        k = k_ref[sl, :]
        u_out = [[None] * ML_HEADS for _ in range(2)]
        for h in range(ML_HEADS):
            vth = vt_ref[h * ML_VT_HEAD:(h + 1) * ML_VT_HEAD, sl].astype(F32)
            k_pair = k[:, (h // 2) * L:(h // 2 + 1) * L]
            for d in range(2):
                ch = d * ML_HEADS + h
                vw = (vth * ws[ch:ch + 1]).astype(BF16)
                u_out[d][h] = jnp.dot(vw, k_pair, preferred_element_type=F32)
        for d in range(2):
            for p in range(ML_HEADS // 2):
                u_ref[ci, d, p] = jnp.where(lane < ML_DIM, u_out[d][2 * p], u_out[d][2 * p + 1])
        qt = qt_ref[:, sl]
        for h in range(ML_HEADS):
            kh = jnp.where(lane_k // ML_DIM == h, k, jnp.zeros_like(k))
            raw_scores.append(jnp.dot(kh, qt, preferred_element_type=F32))
    for ci, sl in enumerate(slices):
        for h in range(ML_HEADS):
            sc_t = raw_scores[ci * ML_HEADS + h]
            vth = vt_ref[h * ML_VT_HEAD:(h + 1) * ML_VT_HEAD, sl]
            for d in range(2):
                ch = d * ML_HEADS + h
                row = ci * _N_CHAINS + ch
                valid = upper if d == 0 else lower
                dm_t = jnp.where(valid, b_all[row:row + 1] + bv_cols[:, row:row + 1], NEG_BIG)
                dmax = jnp.max(dm_t, axis=0, keepdims=True)
                sp_t = (sc_t * jnp.exp(dm_t - dmax)).astype(BF16)
                a_ext = jnp.dot(vth, sp_t, preferred_element_type=F32)
                a_ref = af_ref if d == 0 else ab_ref
                a_ref[h * ML_DIM:(h + 1) * ML_DIM, sl] = a_ext[:ML_DIM]
                aux_ref[_N_CHAINS + ch:_N_CHAINS + ch + 1, sl] = dmax
                aux_ref[2 * _N_CHAINS + ch:2 * _N_CHAINS + ch + 1, sl] = a_ext[ML_DIM:ML_DIM + 1]


def _mlstm_intra(qmt, km, vmt, gt, *, chunks=8):
    L = ML_CHUNK
    t_tokens = km.shape[0]
    w = chunks * L
    col = lambda rows: pl.BlockSpec((rows, w), lambda i: (0, i))
    return pl.pallas_call(
        functools.partial(_mlstm_intra_kernel, chunks=chunks),
        grid=(t_tokens // w,),
        in_specs=[col(ML_WIDTH), pl.BlockSpec((w, ML_WIDTH), lambda i: (i, 0)), col(ML_VT_ROWS), col(N_GATES)],
        out_specs=[col(ML_WIDTH), col(ML_WIDTH), col(_AUX_ROWS),
                   pl.BlockSpec((chunks, 2, ML_HEADS // 2, ML_VT_HEAD, L), lambda i: (i, 0, 0, 0, 0))],
        out_shape=[jax.ShapeDtypeStruct((ML_WIDTH, t_tokens), F32),
                   jax.ShapeDtypeStruct((ML_WIDTH, t_tokens), F32),
                   jax.ShapeDtypeStruct((_AUX_ROWS, t_tokens), F32),
                   jax.ShapeDtypeStruct((t_tokens // L, 2, ML_HEADS // 2, ML_VT_HEAD, L), F32)],
        compiler_params=pltpu.CompilerParams(dimension_semantics=("parallel",)),
        name="mlstm_intra",
    )(qmt, km, vmt, gt)


def _mlstm_scan_kernel(qtf_ref, qtb_ref, af_ref, ab_ref, auxf_ref, auxb_ref, uf_ref, ub_ref,
                       hf_ref, hb_ref, c_ref, m_ref, *, chunks):
    L = ML_CHUNK

    @pl.when(pl.program_id(1) == 0)
    def _():
        c_ref[...] = jnp.zeros_like(c_ref)
        m_ref[...] = jnp.zeros_like(m_ref)

    lane = lax.broadcasted_iota(jnp.int32, (ML_VT_HEAD, L), 1)
    lane_row = lax.broadcasted_iota(jnp.int32, (1, L), 1)
    fwd_rows = lax.broadcasted_iota(jnp.int32, (_N_CHAINS, L), 0) < ML_HEADS
    for i in range(chunks):
        pos = (i, chunks - 1 - i)
        sls = tuple(slice(q * L, (q + 1) * L) for q in pos)

        def aux(kind):
            rows = slice(kind * _N_CHAINS, (kind + 1) * _N_CHAINS)
            return jnp.where(fwd_rows, auxf_ref[rows, sls[0]], auxb_ref[rows, sls[1]])

        b, dmax, den_intra, b_last, gmax = (aux(kind) for kind in range(_AUX_KINDS))
        m_old = m_ref[...]
        m_inter = b + m_old
        m_j = jnp.maximum(m_inter, dmax)
        e_inter = jnp.exp(m_inter - m_j)
        e_intra = jnp.exp(dmax - m_j)
        e_floor = jnp.exp(-m_j)
        m_new = jnp.maximum(b_last + m_old, gmax)
        decay = jnp.exp(b_last + m_old - m_new)
        gain = jnp.exp(gmax - m_new)
        m_ref[...] = m_new
        dirs = ((qtf_ref, af_ref, uf_ref, hf_ref), (qtb_ref, ab_ref, ub_ref, hb_ref))
        for d, (qt_ref, a_ref, u_ref, h_ref) in enumerate(dirs):
            sl = sls[d]
            for p in range(ML_HEADS // 2):
                cp = c_ref[d, p]
                qtp = qt_ref[p * L:(p + 1) * L, sl]
                for hh in range(2):
                    h = 2 * p + hh
                    ch = d * ML_HEADS + h
                    own = lane < ML_DIM if hh == 0 else lane >= ML_DIM
                    cm = jnp.where(own, cp, 0.0).astype(BF16)
                    x_t = jnp.dot(cm, qtp, preferred_element_type=F32)
                    rows = slice(h * ML_DIM, (h + 1) * ML_DIM)
                    num = e_inter[ch:ch + 1] * x_t[:ML_DIM] + e_intra[ch:ch + 1] * a_ref[rows, sl]
                    den = (e_inter[ch:ch + 1] * x_t[ML_DIM:ML_DIM + 1]
                           + e_intra[ch:ch + 1] * den_intra[ch:ch + 1])
                    h_ref[rows, sl] = num / jnp.maximum(jnp.abs(den), e_floor[ch:ch + 1])
                lo, hi_ = d * ML_HEADS + 2 * p, d * ML_HEADS + 2 * p + 1
                decay_p = jnp.where(lane_row < ML_DIM, decay[lo:lo + 1], decay[hi_:hi_ + 1])
                gain_p = jnp.where(lane_row < ML_DIM, gain[lo:lo + 1], gain[hi_:hi_ + 1])
                c_ref[d, p] = decay_p * cp + gain_p * u_ref[pos[d], p]


def _mlstm_scan(qmt, a_f, a_b, aux, u, batch, seq, *, chunks=8):
    L = ML_CHUNK
    w = chunks * L
    nb = seq // w
    t_tokens = batch * seq
    fwd = lambda b, c: b * nb + c
    bwd = lambda b, c: b * nb + (nb - 1 - c)
    col = lambda f, rows: pl.BlockSpec((rows, w), lambda b, c: (0, f(b, c)))
    ublk = lambda f, d: pl.BlockSpec((chunks, None, ML_HEADS // 2, ML_VT_HEAD, L),
                                     lambda b, c: (f(b, c), d, 0, 0, 0))
    return pl.pallas_call(
        functools.partial(_mlstm_scan_kernel, chunks=chunks),
        grid=(batch, nb),
        in_specs=[col(fwd, ML_WIDTH), col(bwd, ML_WIDTH), col(fwd, ML_WIDTH), col(bwd, ML_WIDTH),
                  col(fwd, _AUX_ROWS), col(bwd, _AUX_ROWS), ublk(fwd, 0), ublk(bwd, 1)],
        out_specs=[col(fwd, ML_WIDTH), col(bwd, ML_WIDTH)],
        out_shape=[jax.ShapeDtypeStruct((ML_WIDTH, t_tokens), F32)] * 2,
        scratch_shapes=[pltpu.VMEM((2, ML_HEADS // 2, ML_VT_HEAD, L), F32), pltpu.VMEM((_N_CHAINS, L), F32)],
        compiler_params=pltpu.CompilerParams(dimension_semantics=("parallel", "arbitrary")),
        name="mlstm_scan",
    )(qmt, qmt, a_f, a_b, aux, aux, u, u)


def _out_ffn_kernel(x_ref, yat_ref, yp_ref, hf_ref, hb_ref, om_ref, ng_ref,
                    woa_ref, wop_ref, wom_ref, l1g_ref, l1b_ref, wg_ref, wu_ref, wd_ref,
                    l2g_ref, l2b_ref, o_ref):
    x = x_ref[...]
    parts = []
    for h in range(ML_HEADS):
        rows = slice(h * ML_DIM, (h + 1) * ML_DIM)
        ht = hf_ref[rows, :] + hb_ref[rows, :]
        hc = ht - jnp.mean(ht, axis=0, keepdims=True)
        var = jnp.mean(hc * hc, axis=0, keepdims=True)
        parts.append(hc * lax.rsqrt(var + LN_EPS))
    hn = jnp.concatenate(parts, axis=0).T * ng_ref[...]
    y_ml = (hn / (1.0 + jnp.exp(-om_ref[...]))).astype(BF16)
    y_at = yat_ref[...].T
    mix = (jnp.dot(y_at, woa_ref[...], preferred_element_type=F32)
           + jnp.dot(yp_ref[...], wop_ref[...], preferred_element_type=F32)
           + jnp.dot(y_ml, wom_ref[...], preferred_element_type=F32))
    x1 = _layer_norm(ALPHA * x + mix, l1g_ref[...], l1b_ref[...])
    x1b = x1.astype(BF16)
    gate = jnp.dot(x1b, wg_ref[...], preferred_element_type=F32)
    up = jnp.dot(x1b, wu_ref[...], preferred_element_type=F32)
    hid = (gate / (1.0 + jnp.exp(-gate)) * up).astype(BF16)
    ffn = jnp.dot(hid, wd_ref[...], preferred_element_type=F32)
    o_ref[...] = _layer_norm(ALPHA * x1 + ffn, l2g_ref[...], l2b_ref[...])


def _out_ffn(x, yat, yp, hf, hb, om, lw, *, tm=512):
    t_tokens = x.shape[0]
    row = lambda w: pl.BlockSpec((tm, w), lambda i: (i, 0))
    colt = lambda r: pl.BlockSpec((r, tm), lambda i: (0, i))
    single = lambda a: pl.BlockSpec(a.shape, lambda i: (0,) * a.ndim, pipeline_mode=pl.Buffered(1))
    consts = [lw["norm_g"], lw["wo_a"], lw["wo_p"], lw["wo_m"], lw["ln1_g"], lw["ln1_b"],
              lw["w_gate"], lw["w_up"], lw["w_down"], lw["ln2_g"], lw["ln2_b"]]
    return pl.pallas_call(
        _out_ffn_kernel,
        grid=(t_tokens // tm,),
        in_specs=[row(D_MODEL), colt(N_HEADS * D_V), row(POOL_WIDTH),
                  colt(ML_WIDTH), colt(ML_WIDTH), row(ML_WIDTH)] + [single(c) for c in consts],
        out_specs=row(D_MODEL),
        out_shape=jax.ShapeDtypeStruct((t_tokens, D_MODEL), F32),
        compiler_params=pltpu.CompilerParams(dimension_semantics=("parallel",),
                                             vmem_limit_bytes=VMEM_LIMIT),
        name="out_ffn",
    )(x, yat, yp, hf, hb, om, *consts)


def _head_pad_cols(w, n_heads, width, dst_lo):
    k = w.shape[0]
    w3 = w.reshape(k, n_heads, width)
    out = jnp.zeros((k, n_heads, D_HEAD_PAD), w.dtype)
    out = out.at[:, :, dst_lo:dst_lo + width].set(w3)
    return out.reshape(k, n_heads * D_HEAD_PAD)


def _prep_layer_weights(l, w_in, q_norm_g, w_uq, kv_norm_g, w_ukv, w_pool, pool_scale, gate_bias,
                        norm_g, w_out, ln1_g, ln1_b, w_gate, w_up, w_down, ln2_g, ln2_b):
    wi = w_in[l]
    c = np.cumsum((0, 256, 128, 32, 256, 256, 256, 256, 256, 16))
    w_cq, w_ckv, w_kr, w_pl, w_qm, w_km, w_vm, w_om, w_g = (wi[:, c[j]:c[j + 1]] for j in range(9))
    kr_pad = jnp.zeros((D_MODEL, D_HEAD_PAD), F32).at[:, D_NOPE:D_NOPE + D_ROPE].set(w_kr)
    wa = jnp.concatenate([w_cq, w_ckv, kr_pad, w_pl, w_km * ML_DIM ** -0.5, w_om],
                         axis=1).astype(BF16)
    vm_t = jnp.zeros((ML_HEADS, ML_VT_HEAD, D_MODEL), F32).at[:, :ML_DIM, :].set(
        w_vm.T.reshape(ML_HEADS, ML_DIM, D_MODEL)).reshape(ML_VT_ROWS, D_MODEL)
    gate_order = np.array([0, 2, 1, 3])[:, None] * ML_HEADS + np.arange(ML_HEADS)[None, :]
    gate_order = gate_order.reshape(-1)
    wt = jnp.concatenate([w_qm.T, vm_t, w_g.T[gate_order]], axis=0).astype(BF16)
    uq = w_uq[l].reshape(Q_RANK, N_HEADS, D_NOPE + D_ROPE)
    uq_plain = jnp.zeros((Q_RANK, N_HEADS, D_HEAD_PAD), F32).at[:, :, :D_NOPE + D_ROPE].set(uq)
    wq2 = uq_plain.reshape(Q_RANK, -1).T.astype(BF16)
    ukv = w_ukv[l].reshape(KV_RANK, N_HEADS, D_NOPE + D_V)
    wuk = _head_pad_cols(ukv[:, :, :D_NOPE].reshape(KV_RANK, -1), N_HEADS, D_NOPE, 0).astype(BF16)
    wuvt = ukv[:, :, D_NOPE:].reshape(KV_RANK, -1).T.astype(BF16)
    w_bd = jnp.zeros((POOL_WIDTH, POOL_WIDTH), F32)
    for g in range(len(POOL_WINDOWS)):
        s = slice(g * POOL_GROUP, (g + 1) * POOL_GROUP)
        w_bd = w_bd.at[s, s].set(w_pool[l, g])
    wo = w_out[l].astype(BF16)
    a_end = N_HEADS * D_V
    return {
        "wa": wa, "wt": wt, "gate_bias": gate_bias[l][gate_order].reshape(N_GATES, 1),
        "q_norm_g": q_norm_g[l].reshape(1, -1), "wq2": wq2, "kv_norm_g": kv_norm_g[l].reshape(1, -1),
        "wuk": wuk, "wuvt": wuvt, "w_pool": w_bd.astype(BF16), "pool_scale": pool_scale[l].reshape(1, -1),
        "norm_g": norm_g[l].reshape(1, -1),
        "wo_a": wo[:a_end], "wo_p": wo[a_end:a_end + POOL_WIDTH], "wo_m": wo[a_end + POOL_WIDTH:],
        "ln1_g": ln1_g[l].reshape(1, -1), "ln1_b": ln1_b[l].reshape(1, -1),
        "w_gate": w_gate[l].astype(BF16), "w_up": w_up[l].astype(BF16), "w_down": w_down[l].astype(BF16),
        "ln2_g": ln2_g[l].reshape(1, -1), "ln2_b": ln2_b[l].reshape(1, -1),
    }


def _rope_tables(seq):
    half = D_ROPE // 2
    inv = 1.0 / (ROPE_THETA ** (jnp.arange(0, D_ROPE, 2, dtype=F32) / D_ROPE))
    ang = jnp.arange(seq, dtype=F32)[:, None] * inv[None, :]
    cos, sin = jnp.cos(ang), jnp.sin(ang)
    one = jnp.ones((seq, D_NOPE), F32)
    zero_n = jnp.zeros((seq, D_NOPE), F32)
    zero_p = jnp.zeros((seq, D_HEAD_PAD - D_NOPE - D_ROPE), F32)
    q_scale = (D_NOPE + D_ROPE) ** -0.5 * LOG2E
    cq = jnp.concatenate([one, cos, cos, zero_p], axis=1) * q_scale
    sq = jnp.concatenate([zero_n, sin, sin, zero_p], axis=1) * q_scale
    ck = jnp.concatenate([zero_n, cos, cos, zero_p], axis=1)
    sk = jnp.concatenate([zero_n, sin, sin, zero_p], axis=1)
    return cq.T, sq.T, ck, sk


def _trunk(x, ln_in, layers):
    batch, seq, _ = x.shape
    tabs = _rope_tables(seq)
    xf = x.reshape(batch * seq, D_MODEL)
    for l, lw in enumerate(layers):
        outs = _in_proj(xf, lw, tabs, seq, ln=ln_in if l == 0 else None)
        if l == 0:
            xf, outs = outs[0], outs[1:]
        q, k, vt, xp, km, om, qmt, vmt, gt = outs
        yat = _attention(q, k, vt, batch, seq)
        yp = _pool(xp, lw["w_pool"], lw["pool_scale"], batch, seq)
        a_f, a_b, aux, u = _mlstm_intra(qmt, km, vmt, gt)
        hf, hb = _mlstm_scan(qmt, a_f, a_b, aux, u, batch, seq)
        xf = _out_ffn(xf, yat, yp, hf, hb, om, lw)
    return xf.reshape(batch, seq, D_MODEL)


def kernel(x_prompt, x_sample, ln_in_g, ln_in_b, w_in, q_norm_g, w_uq, kv_norm_g, w_ukv, w_pool, pool_scale,
           mlstm_gate_bias, mlstm_norm_g, w_out, ln1_g, ln1_b, w_gate, w_up, w_down, ln2_g, ln2_b):
    layers = [_prep_layer_weights(l, w_in, q_norm_g, w_uq, kv_norm_g, w_ukv, w_pool, pool_scale,
                                  mlstm_gate_bias, mlstm_norm_g, w_out, ln1_g, ln1_b, w_gate, w_up, w_down,
                                  ln2_g, ln2_b) for l in range(DEPTH)]
    ln_in = (ln_in_g.reshape(1, -1), ln_in_b.reshape(1, -1))
    return (_trunk(x_prompt, ln_in, layers), _trunk(x_sample, ln_in, layers))
```

```python
import functools
import math

import jax
import jax.numpy as jnp
import numpy as np
from jax import lax
from jax.experimental import pallas as pl
from jax.experimental.pallas import tpu as pltpu

F32 = jnp.float32
BF16 = jnp.bfloat16

D_MODEL = 1024
DEPTH = 4
N_HEADS = 8
D_NOPE = 64
D_ROPE = 32
D_V = 64
BF16_ROWS = 16
ACC_ROWS = 72
Q_RANK = 256
KV_RANK = 128
ROPE_THETA = 10000.0
D_HEAD_PAD = 128
POOL_WIDTH = 256
POOL_WINDOWS = (2, 4, 8, 16)
POOL_GROUP = 64
POOL_HALO = 8
ML_HEADS = 4
ML_DIM = 64
ML_WIDTH = 256
ML_CHUNK = 128
ML_VT_HEAD = 80
ML_VT_ROWS = ML_HEADS * ML_VT_HEAD
N_GATES = 16
NEG_BIG = -1e30
D_FF = 2816
ALPHA = (2 * DEPTH) ** 0.25
LN_EPS = 1e-5
LOG2E = 1.4426950408889634

VMEM_LIMIT = 56 * 1024 * 1024

_PA_CQ, _PA_CKV, _PA_KR, _PA_POOL, _PA_KM, _PA_OM, _PA_END = (0, 256, 384, 512, 768, 1024, 1280)

_NT = (((1,), (1,)), ((), ()))


def _layer_norm(x, g, b):
    mu = jnp.mean(x, axis=-1, keepdims=True)
    xc = x - mu
    var = jnp.mean(xc * xc, axis=-1, keepdims=True)
    return xc * lax.rsqrt(var + LN_EPS) * g + b


def _rms_norm(x, g):
    return x * lax.rsqrt(jnp.mean(x * x, axis=-1, keepdims=True) + LN_EPS) * g


def _in_proj_kernel(*refs, apply_ln):
    if apply_ln:
        (x_ref, lng_ref, lnb_ref, wa_ref, wt_ref, gb_ref, qg_ref, wq_ref, kvg_ref, wuk_ref, wuvt_ref,
         cq_ref, sq_ref, ck_ref, sk_ref,
         xn_ref, q_ref, k_ref, vt_ref, pool_ref, km_ref, om_ref, qmt_ref, vmt_ref, gt_ref) = refs
    else:
        (x_ref, wa_ref, wt_ref, gb_ref, qg_ref, wq_ref, kvg_ref, wuk_ref, wuvt_ref,
         cq_ref, sq_ref, ck_ref, sk_ref,
         q_ref, k_ref, vt_ref, pool_ref, km_ref, om_ref, qmt_ref, vmt_ref, gt_ref) = refs
    x = x_ref[...]
    if apply_ln:
        x = _layer_norm(x, lng_ref[...], lnb_ref[...])
        xn_ref[...] = x
    xb = x.astype(BF16)
    pa = jnp.dot(xb, wa_ref[...], preferred_element_type=F32)
    pt = lax.dot_general(wt_ref[...], xb, _NT, preferred_element_type=F32)
    qmt_ref[...] = pt[:ML_WIDTH].astype(BF16)
    vrow = lax.broadcasted_iota(jnp.int32, (ML_VT_ROWS, 1), 0) % ML_VT_HEAD
    vmt_ref[...] = (pt[ML_WIDTH:ML_WIDTH + ML_VT_ROWS] + (vrow == ML_DIM).astype(F32)).astype(BF16)
    gt_ref[...] = pt[ML_WIDTH + ML_VT_ROWS:] + gb_ref[...]

    pool_ref[...] = pa[:, _PA_POOL:_PA_KM]
    km_ref[...] = pa[:, _PA_KM:_PA_OM].astype(BF16)
    om_ref[...] = pa[:, _PA_OM:_PA_END]

    cqn = _rms_norm(pa[:, _PA_CQ:_PA_CKV], qg_ref[...]).astype(BF16)
    qt = lax.dot_general(wq_ref[...], cqn, _NT, preferred_element_type=F32)
    cq, sq = cq_ref[...], sq_ref[...]
    x1, x2, half = D_NOPE, D_NOPE + D_ROPE // 2, D_ROPE // 2
    for h in range(N_HEADS):
        lo = h * D_HEAD_PAD
        qh = qt[lo:lo + D_HEAD_PAD]
        partner = jnp.concatenate([qh[:x1], -qh[x2:x2 + half], qh[x1:x2], qh[x2 + half:]], axis=0)
        q_ref[lo:lo + D_HEAD_PAD, :] = (qh * cq + partner * sq).astype(BF16)

    ckvn = _rms_norm(pa[:, _PA_CKV:_PA_KR], kvg_ref[...]).astype(BF16)
    kn = jnp.dot(ckvn, wuk_ref[...], preferred_element_type=F32)
    kr_raw = pa[:, _PA_KR:_PA_POOL]
    lane = lax.broadcasted_iota(jnp.int32, (1, D_HEAD_PAD), 1)
    kr_partner = jnp.where(lane < x2, -pltpu.roll(kr_raw, D_HEAD_PAD - half, axis=1),
                           pltpu.roll(kr_raw, half, axis=1))
    kr = kr_raw * ck_ref[...] + kr_partner * sk_ref[...]
    for h in range(N_HEADS):
        lo = h * D_HEAD_PAD
        k_ref[:, lo:lo + D_HEAD_PAD] = (kn[:, lo:lo + D_HEAD_PAD] + kr).astype(BF16)
    vt = lax.dot_general(wuvt_ref[...], ckvn, _NT, preferred_element_type=F32)
    vt_ref[...] = vt.astype(BF16)


def _const_spec(shape):
    nd = len(shape)
    return pl.BlockSpec(shape, lambda *_: (0,) * nd)


def _in_proj(x, lw, tabs, seq, *, ln=None, tm=512):
    t_tokens = x.shape[0]
    n_seq_tiles = seq // tm
    apply_ln = ln is not None
    row = lambda w: pl.BlockSpec((tm, w), lambda i: (i, 0))
    colt = lambda r: pl.BlockSpec((r, tm), lambda i: (0, i))
    tab = pl.BlockSpec((tm, D_HEAD_PAD), lambda i: (i % n_seq_tiles, 0))
    consts = [lw["wa"], lw["wt"], lw["gate_bias"], lw["q_norm_g"], lw["wq2"], lw["kv_norm_g"],
              lw["wuk"], lw["wuvt"]]
    in_specs = [row(D_MODEL)]
    args = [x]
    if apply_ln:
        in_specs += [_const_spec((1, D_MODEL))] * 2
        args += [ln[0], ln[1]]
    tab_t = pl.BlockSpec((D_HEAD_PAD, tm), lambda i: (0, i % n_seq_tiles))
    in_specs += [_const_spec(c.shape) for c in consts] + [tab_t, tab_t, tab, tab]
    args += consts + list(tabs)
    hw = N_HEADS * D_HEAD_PAD
    out_shape, out_specs = [], []
    if apply_ln:
        out_shape.append(jax.ShapeDtypeStruct((t_tokens, D_MODEL), F32))
        out_specs.append(row(D_MODEL))
    out_shape += [
        jax.ShapeDtypeStruct((hw, t_tokens), BF16),
        jax.ShapeDtypeStruct((t_tokens, hw), BF16),
        jax.ShapeDtypeStruct((N_HEADS * D_V, t_tokens), BF16),
        jax.ShapeDtypeStruct((t_tokens, POOL_WIDTH), F32),
        jax.ShapeDtypeStruct((t_tokens, ML_WIDTH), BF16),
        jax.ShapeDtypeStruct((t_tokens, ML_WIDTH), F32),
        jax.ShapeDtypeStruct((ML_WIDTH, t_tokens), BF16),
        jax.ShapeDtypeStruct((ML_VT_ROWS, t_tokens), BF16),
        jax.ShapeDtypeStruct((N_GATES, t_tokens), F32),
    ]
    out_specs += [colt(hw), row(hw), colt(N_HEADS * D_V), row(POOL_WIDTH), row(ML_WIDTH), row(ML_WIDTH),
                  colt(ML_WIDTH), colt(ML_VT_ROWS), colt(N_GATES)]
    return pl.pallas_call(
        functools.partial(_in_proj_kernel, apply_ln=apply_ln),
        grid=(t_tokens // tm,),
        in_specs=in_specs,
        out_specs=out_specs,
        out_shape=out_shape,
        compiler_params=pltpu.CompilerParams(dimension_semantics=("parallel",),
                                             vmem_limit_bytes=VMEM_LIMIT),
        name="in_proj_ln" if apply_ln else "in_proj",
    )(*args)


def _attention_kernel(q_ref, k_ref, vt_ref, o_ref, s_ref, smax_ref, acc_ref, m_ref, *, tq, tk, sub, unroll):
    n_qt = q_ref.shape[1] // tq
    n_sub = tq // sub
    n_k = k_ref.shape[0] // tk
    assert unroll % 2 == 0 and n_k % 2 == 0

    def put_scores(qi, t, slot, c):
        lo = qi * tq + c * sub
        kblk = k_ref[pl.ds(pl.multiple_of(t * tk, tk), tk), :]
        s = jnp.dot(kblk, q_ref[:, lo:lo + sub], preferred_element_type=F32)
        s_ref[slot, c] = s
        smax_ref[slot, :, c * sub:(c + 1) * sub] = jnp.max(s, axis=0, keepdims=True)

    def step(qi, t, parity, nxt):
        vblk = vt_ref[:, pl.ds(pl.multiple_of(t * tk, tk), tk)]
        vext = jnp.concatenate([vblk, jnp.ones((BF16_ROWS, tk), BF16)], axis=0)
        for c in range(n_sub):
            if nxt is not None:
                put_scores(nxt[0], nxt[1], 1 - parity, c)
            cols = slice(qi * tq + c * sub, qi * tq + (c + 1) * sub)
            s = s_ref[parity, c]
            m = m_ref[:, cols]
            m_new = jnp.maximum(m, smax_ref[parity, :, c * sub:(c + 1) * sub])
            alpha = jnp.exp2(m - m_new)
            p = jnp.exp2(s - m_new).astype(BF16)
            pv = jnp.dot(vext, p, preferred_element_type=F32)
            acc_ref[:, cols] = alpha * acc_ref[:, cols] + pv[:ACC_ROWS]
            m_ref[:, cols] = m_new

    m_ref[...] = jnp.full(m_ref.shape, NEG_BIG, F32)
    acc_ref[...] = jnp.zeros_like(acc_ref)
    for c in range(n_sub):
        put_scores(0, 0, 0, c)
    trips = (n_k - 1) // unroll
    for qi in range(n_qt):
        def body(j, _, qi=qi):
            for i in range(unroll):
                t = unroll * j + i
                step(qi, t, i % 2, (qi, t + 1))
            return 0

        if trips:
            lax.fori_loop(0, trips, body, 0)
        for t in range(trips * unroll, n_k):
            nxt = (qi, t + 1) if t + 1 < n_k else ((qi + 1, 0) if qi + 1 < n_qt else None)
            step(qi, t, t % 2, nxt)
    o_ref[...] = (acc_ref[:D_V, :] / acc_ref[D_V:D_V + 1, :]).astype(BF16)


def _attention(q, k, vt, batch, seq, *, tq=1024, q_tiles=2, tk=512, sub=256, unroll=10):
    wq = tq * q_tiles
    assert seq % wq == 0 and seq % (2 * tk) == 0
    n_q = seq // wq
    t_tokens = batch * seq
    k3 = k.reshape(batch, seq, N_HEADS * D_HEAD_PAD)
    return pl.pallas_call(
        functools.partial(_attention_kernel, tq=tq, tk=tk, sub=sub, unroll=unroll),
        grid=(batch, N_HEADS, n_q),
        in_specs=[
            pl.BlockSpec((D_HEAD_PAD, wq), lambda b, h, i: (h, b * n_q + i)),
            pl.BlockSpec((None, seq, D_HEAD_PAD), lambda b, h, i: (b, 0, h)),
            pl.BlockSpec((D_V, seq), lambda b, h, i: (h, b)),
        ],
        out_specs=pl.BlockSpec((D_V, wq), lambda b, h, i: (h, b * n_q + i)),
        out_shape=jax.ShapeDtypeStruct((N_HEADS * D_V, t_tokens), BF16),
        scratch_shapes=[pltpu.VMEM((2, tq // sub, tk, sub), F32), pltpu.VMEM((2, 1, tq), F32),
                        pltpu.VMEM((ACC_ROWS, wq), F32), pltpu.VMEM((1, wq), F32)],
        compiler_params=pltpu.CompilerParams(
            dimension_semantics=("parallel", "parallel", "arbitrary"), vmem_limit_bytes=VMEM_LIMIT),
        name="attention",
    )(q, k3, vt)


def _pool_kernel(x_ref, prev_ref, next_ref, w_ref, scale_ref, o_ref, *, seq):
    i = pl.program_id(1)
    ts = x_ref.shape[0]
    x = x_ref[...]
    prev = jnp.where(i == 0, 0.0, prev_ref[...])
    nxt = jnp.where(i == pl.num_programs(1) - 1, 0.0, next_ref[...])
    xe = jnp.concatenate([prev, x, nxt], axis=0)
    n = ts + 2 * POOL_HALO
    half_w = POOL_WIDTH // 2
    lane = lax.broadcasted_iota(jnp.int32, (1, half_w), 1)
    t = i * ts + lax.broadcasted_iota(jnp.int32, (ts, 1), 0)
    halves = []
    for hf in range(2):
        xh = xe[:, hf * half_w:(hf + 1) * half_w]
        w_lo, w_hi = POOL_WINDOWS[2 * hf], POOL_WINDOWS[2 * hf + 1]
        p, w, length = xh, 1, n
        sums = {}
        while w < w_hi:
            length -= w
            p = p[0:length] + p[w:w + length]
            w *= 2
            sums[w] = p
        wins = []
        for wn in (w_lo, w_hi):
            start = POOL_HALO - wn // 2
            wins.append(sums[wn][start:start + ts])
        first = lane < POOL_GROUP
        win = jnp.where(first, wins[0], wins[1])
        hw = jnp.where(first, w_lo // 2, w_hi // 2)
        cnt = (jnp.minimum(t + hw, seq) - jnp.maximum(t - hw, 0)).astype(F32)
        halves.append(win / cnt - x[:, hf * half_w:(hf + 1) * half_w])
    y = jnp.concatenate(halves, axis=1)
    out = jnp.dot(y.astype(BF16), w_ref[...], preferred_element_type=F32) * scale_ref[...]
    o_ref[...] = out.astype(BF16)


def _pool(xp, w_bd, scale, batch, seq, *, ts=512):
    n_t = seq // ts
    hb = ts // POOL_HALO
    x3 = xp.reshape(batch, seq, POOL_WIDTH)
    out = pl.pallas_call(
        functools.partial(_pool_kernel, seq=seq),
        grid=(batch, n_t),
        in_specs=[
            pl.BlockSpec((None, ts, POOL_WIDTH), lambda b, i: (b, i, 0)),
            pl.BlockSpec((None, POOL_HALO, POOL_WIDTH), lambda b, i: (b, jnp.maximum(i * hb - 1, 0), 0)),
            pl.BlockSpec((None, POOL_HALO, POOL_WIDTH),
                         lambda b, i: (b, jnp.minimum((i + 1) * hb, seq // POOL_HALO - 1), 0)),
            _const_spec((POOL_WIDTH, POOL_WIDTH)),
            _const_spec((1, POOL_WIDTH)),
        ],
        out_specs=pl.BlockSpec((None, ts, POOL_WIDTH), lambda b, i: (b, i, 0)),
        out_shape=jax.ShapeDtypeStruct((batch, seq, POOL_WIDTH), BF16),
        compiler_params=pltpu.CompilerParams(dimension_semantics=("parallel", "parallel")),
        name="pool",
    )(x3, x3, x3, w_bd, scale)
    return out.reshape(batch * seq, POOL_WIDTH)


def _log_sigmoid(x):
    return -(jnp.maximum(-x, 0.0) + jnp.log1p(jnp.exp(-jnp.abs(x))))


_N_CHAINS = 2 * ML_HEADS
_AUX_KINDS = 5
_AUX_ROWS = _AUX_KINDS * _N_CHAINS


def _mlstm_intra_kernel(qt_ref, k_ref, vt_ref, gt_ref, af_ref, ab_ref, aux_ref, u_ref, *, chunks):
    L = ML_CHUNK
    r = lax.broadcasted_iota(jnp.int32, (L, L), 0)
    c = lax.broadcasted_iota(jnp.int32, (L, L), 1)
    lower, upper = c <= r, c >= r
    up_f = upper.astype(F32)
    lane_k = lax.broadcasted_iota(jnp.int32, (L, ML_WIDTH), 1)
    lane = lax.broadcasted_iota(jnp.int32, (ML_VT_HEAD, L), 1)
    slices = [slice(ci * L, (ci + 1) * L) for ci in range(chunks)]
    n_rows = chunks * _N_CHAINS
    gi = jnp.concatenate([gt_ref[:_N_CHAINS, sl] for sl in slices], axis=0)
    logf = _log_sigmoid(jnp.concatenate([gt_ref[_N_CHAINS:, sl] for sl in slices], axis=0))
    fwd_rows = lax.broadcasted_iota(jnp.int32, (n_rows, L), 0) % _N_CHAINS < ML_HEADS
    prefix = jnp.dot(logf, up_f, precision=lax.Precision.HIGHEST, preferred_element_type=F32)
    total = prefix[:, L - 1:L]
    b_all = jnp.where(fwd_rows, prefix, (total - prefix) + logf)
    bv = gi - b_all
    g = total + bv
    gmax = jnp.max(g, axis=1, keepdims=True)
    ws_all = jnp.exp(g - gmax)
    bv_cols = jnp.concatenate([bv, jnp.zeros((L - n_rows, L), F32)], axis=0).T
    total_b = jnp.broadcast_to(total, (n_rows, L))
    gmax_b = jnp.broadcast_to(gmax, (n_rows, L))
    raw_scores = []
    for ci, sl in enumerate(slices):
        rows = slice(ci * _N_CHAINS, (ci + 1) * _N_CHAINS)
        ws = ws_all[rows]
        aux_ref[0:_N_CHAINS, sl] = b_all[rows]
        aux_ref[3 * _N_CHAINS:4 * _N_CHAINS, sl] = total_b[rows]
        aux_ref[4 * _N_CHAINS:5 * _N_CHAINS, sl] = gmax_b[rows]
        k = k_ref[sl, :]
        u_out = [[None] * ML_HEADS for _ in range(2)]
        for h in range(ML_HEADS):
            vth = vt_ref[h * ML_VT_HEAD:(h + 1) * ML_VT_HEAD, sl].astype(F32)
            k_pair = k[:, (h // 2) * L:(h // 2 + 1) * L]
            for d in range(2):
                ch = d * ML_HEADS + h
                vw = (vth * ws[ch:ch + 1]).astype(BF16)
                u_out[d][h] = jnp.dot(vw, k_pair, preferred_element_type=F32)
        for d in range(2):
            for p in range(ML_HEADS // 2):
                u_ref[ci, d, p] = jnp.where(lane < ML_DIM, u_out[d][2 * p], u_out[d][2 * p + 1])
        qt = qt_ref[:, sl]
        for h in range(ML_HEADS):
            kh = jnp.where(lane_k // ML_DIM == h, k, jnp.zeros_like(k))
            raw_scores.append(jnp.dot(kh, qt, preferred_element_type=F32))
    for ci, sl in enumerate(slices):
        for h in range(ML_HEADS):
            sc_t = raw_scores[ci * ML_HEADS + h]
            vth = vt_ref[h * ML_VT_HEAD:(h + 1) * ML_VT_HEAD, sl]
            for d in range(2):
                ch = d * ML_HEADS + h
                row = ci * _N_CHAINS + ch
                valid = upper if d == 0 else lower
                dm_t = jnp.where(valid, b_all[row:row + 1] + bv_cols[:, row:row + 1], NEG_BIG)
                dmax = jnp.max(dm_t, axis=0, keepdims=True)
                sp_t = (sc_t * jnp.exp(dm_t - dmax)).astype(BF16)
                a_ext = jnp.dot(vth, sp_t, preferred_element_type=F32)
                a_ref = af_ref if d == 0 else ab_ref
                a_ref[h * ML_DIM:(h + 1) * ML_DIM, sl] = a_ext[:ML_DIM]
                aux_ref[_N_CHAINS + ch:_N_CHAINS + ch + 1, sl] = dmax
                aux_ref[2 * _N_CHAINS + ch:2 * _N_CHAINS + ch + 1, sl] = a_ext[ML_DIM:ML_DIM + 1]


def _mlstm_intra(qmt, km, vmt, gt, *, chunks=8):
    L = ML_CHUNK
    t_tokens = km.shape[0]
    w = chunks * L
    col = lambda rows: pl.BlockSpec((rows, w), lambda i: (0, i))
    return pl.pallas_call(
        functools.partial(_mlstm_intra_kernel, chunks=chunks),
        grid=(t_tokens // w,),
        in_specs=[col(ML_WIDTH), pl.BlockSpec((w, ML_WIDTH), lambda i: (i, 0)), col(ML_VT_ROWS), col(N_GATES)],
        out_specs=[col(ML_WIDTH), col(ML_WIDTH), col(_AUX_ROWS),
                   pl.BlockSpec((chunks, 2, ML_HEADS // 2, ML_VT_HEAD, L), lambda i: (i, 0, 0, 0, 0))],
        out_shape=[jax.ShapeDtypeStruct((ML_WIDTH, t_tokens), F32),
                   jax.ShapeDtypeStruct((ML_WIDTH, t_tokens), F32),
                   jax.ShapeDtypeStruct((_AUX_ROWS, t_tokens), F32),
                   jax.ShapeDtypeStruct((t_tokens // L, 2, ML_HEADS // 2, ML_VT_HEAD, L), F32)],
        compiler_params=pltpu.CompilerParams(dimension_semantics=("parallel",)),
        name="mlstm_intra",
    )(qmt, km, vmt, gt)


def _mlstm_scan_kernel(qtf_ref, qtb_ref, af_ref, ab_ref, auxf_ref, auxb_ref, uf_ref, ub_ref,
                       hf_ref, hb_ref, c_ref, m_ref, *, chunks):
    L = ML_CHUNK

    @pl.when(pl.program_id(1) == 0)
    def _():
        c_ref[...] = jnp.zeros_like(c_ref)
        m_ref[...] = jnp.zeros_like(m_ref)

    lane = lax.broadcasted_iota(jnp.int32, (ML_VT_HEAD, L), 1)
    lane_row = lax.broadcasted_iota(jnp.int32, (1, L), 1)
    fwd_rows = lax.broadcasted_iota(jnp.int32, (_N_CHAINS, L), 0) < ML_HEADS
    for i in range(chunks):
        pos = (i, chunks - 1 - i)
        sls = tuple(slice(q * L, (q + 1) * L) for q in pos)

        def aux(kind):
            rows = slice(kind * _N_CHAINS, (kind + 1) * _N_CHAINS)
            return jnp.where(fwd_rows, auxf_ref[rows, sls[0]], auxb_ref[rows, sls[1]])

        b, dmax, den_intra, b_last, gmax = (aux(kind) for kind in range(_AUX_KINDS))
        m_old = m_ref[...]
        m_inter = b + m_old
        m_j = jnp.maximum(m_inter, dmax)
        e_inter = jnp.exp(m_inter - m_j)
        e_intra = jnp.exp(dmax - m_j)
        e_floor = jnp.exp(-m_j)
        m_new = jnp.maximum(b_last + m_old, gmax)
        decay = jnp.exp(b_last + m_old - m_new)
        gain = jnp.exp(gmax - m_new)
        m_ref[...] = m_new
        dirs = ((qtf_ref, af_ref, uf_ref, hf_ref), (qtb_ref, ab_ref, ub_ref, hb_ref))
        for d, (qt_ref, a_ref, u_ref, h_ref) in enumerate(dirs):
            sl = sls[d]
            for p in range(ML_HEADS // 2):
                cp = c_ref[d, p]
                qtp = qt_ref[p * L:(p + 1) * L, sl]
                for hh in range(2):
                    h = 2 * p + hh
                    ch = d * ML_HEADS + h
                    own = lane < ML_DIM if hh == 0 else lane >= ML_DIM
                    cm = jnp.where(own, cp, 0.0).astype(BF16)
                    x_t = jnp.dot(cm, qtp, preferred_element_type=F32)
                    rows = slice(h * ML_DIM, (h + 1) * ML_DIM)
                    num = e_inter[ch:ch + 1] * x_t[:ML_DIM] + e_intra[ch:ch + 1] * a_ref[rows, sl]
                    den = (e_inter[ch:ch + 1] * x_t[ML_DIM:ML_DIM + 1]
                           + e_intra[ch:ch + 1] * den_intra[ch:ch + 1])
                    h_ref[rows, sl] = num / jnp.maximum(jnp.abs(den), e_floor[ch:ch + 1])
                lo, hi_ = d * ML_HEADS + 2 * p, d * ML_HEADS + 2 * p + 1
                decay_p = jnp.where(lane_row < ML_DIM, decay[lo:lo + 1], decay[hi_:hi_ + 1])
                gain_p = jnp.where(lane_row < ML_DIM, gain[lo:lo + 1], gain[hi_:hi_ + 1])
                c_ref[d, p] = decay_p * cp + gain_p * u_ref[pos[d], p]


def _mlstm_scan(qmt, a_f, a_b, aux, u, batch, seq, *, chunks=8):
    L = ML_CHUNK
    w = chunks * L
    nb = seq // w
    t_tokens = batch * seq
    fwd = lambda b, c: b * nb + c
    bwd = lambda b, c: b * nb + (nb - 1 - c)
    col = lambda f, rows: pl.BlockSpec((rows, w), lambda b, c: (0, f(b, c)))
    ublk = lambda f, d: pl.BlockSpec((chunks, None, ML_HEADS // 2, ML_VT_HEAD, L),
                                     lambda b, c: (f(b, c), d, 0, 0, 0))
    return pl.pallas_call(
        functools.partial(_mlstm_scan_kernel, chunks=chunks),
        grid=(batch, nb),
        in_specs=[col(fwd, ML_WIDTH), col(bwd, ML_WIDTH), col(fwd, ML_WIDTH), col(bwd, ML_WIDTH),
                  col(fwd, _AUX_ROWS), col(bwd, _AUX_ROWS), ublk(fwd, 0), ublk(bwd, 1)],
        out_specs=[col(fwd, ML_WIDTH), col(bwd, ML_WIDTH)],
        out_shape=[jax.ShapeDtypeStruct((ML_WIDTH, t_tokens), F32)] * 2,
        scratch_shapes=[pltpu.VMEM((2, ML_HEADS // 2, ML_VT_HEAD, L), F32), pltpu.VMEM((_N_CHAINS, L), F32)],
        compiler_params=pltpu.CompilerParams(dimension_semantics=("parallel", "arbitrary")),
        name="mlstm_scan",
    )(qmt, qmt, a_f, a_b, aux, aux, u, u)


def _out_ffn_kernel(x_ref, yat_ref, yp_ref, hf_ref, hb_ref, om_ref, ng_ref,
                    woa_ref, wop_ref, wom_ref, l1g_ref, l1b_ref, wg_ref, wu_ref, wd_ref,
                    l2g_ref, l2b_ref, o_ref):
    x = x_ref[...]
    parts = []
    for h in range(ML_HEADS):
        rows = slice(h * ML_DIM, (h + 1) * ML_DIM)
        ht = hf_ref[rows, :] + hb_ref[rows, :]
        hc = ht - jnp.mean(ht, axis=0, keepdims=True)
        var = jnp.mean(hc * hc, axis=0, keepdims=True)
        parts.append(hc * lax.rsqrt(var + LN_EPS))
    hn = jnp.concatenate(parts, axis=0).T * ng_ref[...]
    y_ml = (hn / (1.0 + jnp.exp(-om_ref[...]))).astype(BF16)
    y_at = yat_ref[...].T
    mix = (jnp.dot(y_at, woa_ref[...], preferred_element_type=F32)
           + jnp.dot(yp_ref[...], wop_ref[...], preferred_element_type=F32)
           + jnp.dot(y_ml, wom_ref[...], preferred_element_type=F32))
    x1 = _layer_norm(ALPHA * x + mix, l1g_ref[...], l1b_ref[...])
    x1b = x1.astype(BF16)
    gate = jnp.dot(x1b, wg_ref[...], preferred_element_type=F32)
    up = jnp.dot(x1b, wu_ref[...], preferred_element_type=F32)
    hid = (gate / (1.0 + jnp.exp(-gate)) * up).astype(BF16)
    ffn = jnp.dot(hid, wd_ref[...], preferred_element_type=F32)
    o_ref[...] = _layer_norm(ALPHA * x1 + ffn, l2g_ref[...], l2b_ref[...])


def _out_ffn(x, yat, yp, hf, hb, om, lw, *, tm=512):
    t_tokens = x.shape[0]
    row = lambda w: pl.BlockSpec((tm, w), lambda i: (i, 0))
    colt = lambda r: pl.BlockSpec((r, tm), lambda i: (0, i))
    single = lambda a: pl.BlockSpec(a.shape, lambda i: (0,) * a.ndim, pipeline_mode=pl.Buffered(1))
    consts = [lw["norm_g"], lw["wo_a"], lw["wo_p"], lw["wo_m"], lw["ln1_g"], lw["ln1_b"],
              lw["w_gate"], lw["w_up"], lw["w_down"], lw["ln2_g"], lw["ln2_b"]]
    return pl.pallas_call(
        _out_ffn_kernel,
        grid=(t_tokens // tm,),
        in_specs=[row(D_MODEL), colt(N_HEADS * D_V), row(POOL_WIDTH),
                  colt(ML_WIDTH), colt(ML_WIDTH), row(ML_WIDTH)] + [single(c) for c in consts],
        out_specs=row(D_MODEL),
        out_shape=jax.ShapeDtypeStruct((t_tokens, D_MODEL), F32),
        compiler_params=pltpu.CompilerParams(dimension_semantics=("parallel",),
                                             vmem_limit_bytes=VMEM_LIMIT),
        name="out_ffn",
    )(x, yat, yp, hf, hb, om, *consts)


def _head_pad_cols(w, n_heads, width, dst_lo):
    k = w.shape[0]
    w3 = w.reshape(k, n_heads, width)
    out = jnp.zeros((k, n_heads, D_HEAD_PAD), w.dtype)
    out = out.at[:, :, dst_lo:dst_lo + width].set(w3)
    return out.reshape(k, n_heads * D_HEAD_PAD)


def _prep_layer_weights(l, w_in, q_norm_g, w_uq, kv_norm_g, w_ukv, w_pool, pool_scale, gate_bias,
                        norm_g, w_out, ln1_g, ln1_b, w_gate, w_up, w_down, ln2_g, ln2_b):
    wi = w_in[l]
    c = np.cumsum((0, 256, 128, 32, 256, 256, 256, 256, 256, 16))
    w_cq, w_ckv, w_kr, w_pl, w_qm, w_km, w_vm, w_om, w_g = (wi[:, c[j]:c[j + 1]] for j in range(9))
    kr_pad = jnp.zeros((D_MODEL, D_HEAD_PAD), F32).at[:, D_NOPE:D_NOPE + D_ROPE].set(w_kr)
    wa = jnp.concatenate([w_cq, w_ckv, kr_pad, w_pl, w_km * ML_DIM ** -0.5, w_om],
                         axis=1).astype(BF16)
    vm_t = jnp.zeros((ML_HEADS, ML_VT_HEAD, D_MODEL), F32).at[:, :ML_DIM, :].set(
        w_vm.T.reshape(ML_HEADS, ML_DIM, D_MODEL)).reshape(ML_VT_ROWS, D_MODEL)
    gate_order = np.array([0, 2, 1, 3])[:, None] * ML_HEADS + np.arange(ML_HEADS)[None, :]
    gate_order = gate_order.reshape(-1)
    wt = jnp.concatenate([w_qm.T, vm_t, w_g.T[gate_order]], axis=0).astype(BF16)
    uq = w_uq[l].reshape(Q_RANK, N_HEADS, D_NOPE + D_ROPE)
    uq_plain = jnp.zeros((Q_RANK, N_HEADS, D_HEAD_PAD), F32).at[:, :, :D_NOPE + D_ROPE].set(uq)
    wq2 = uq_plain.reshape(Q_RANK, -1).T.astype(BF16)
    ukv = w_ukv[l].reshape(KV_RANK, N_HEADS, D_NOPE + D_V)
    wuk = _head_pad_cols(ukv[:, :, :D_NOPE].reshape(KV_RANK, -1), N_HEADS, D_NOPE, 0).astype(BF16)
    wuvt = ukv[:, :, D_NOPE:].reshape(KV_RANK, -1).T.astype(BF16)
    w_bd = jnp.zeros((POOL_WIDTH, POOL_WIDTH), F32)
    for g in range(len(POOL_WINDOWS)):
        s = slice(g * POOL_GROUP, (g + 1) * POOL_GROUP)
        w_bd = w_bd.at[s, s].set(w_pool[l, g])
    wo = w_out[l].astype(BF16)
    a_end = N_HEADS * D_V
    return {
        "wa": wa, "wt": wt, "gate_bias": gate_bias[l][gate_order].reshape(N_GATES, 1),
        "q_norm_g": q_norm_g[l].reshape(1, -1), "wq2": wq2, "kv_norm_g": kv_norm_g[l].reshape(1, -1),
        "wuk": wuk, "wuvt": wuvt, "w_pool": w_bd.astype(BF16), "pool_scale": pool_scale[l].reshape(1, -1),
        "norm_g": norm_g[l].reshape(1, -1),
        "wo_a": wo[:a_end], "wo_p": wo[a_end:a_end + POOL_WIDTH], "wo_m": wo[a_end + POOL_WIDTH:],
        "ln1_g": ln1_g[l].reshape(1, -1), "ln1_b": ln1_b[l].reshape(1, -1),
        "w_gate": w_gate[l].astype(BF16), "w_up": w_up[l].astype(BF16), "w_down": w_down[l].astype(BF16),
        "ln2_g": ln2_g[l].reshape(1, -1), "ln2_b": ln2_b[l].reshape(1, -1),
    }


def _rope_tables(seq):
    half = D_ROPE // 2
    inv = 1.0 / (ROPE_THETA ** (jnp.arange(0, D_ROPE, 2, dtype=F32) / D_ROPE))
    ang = jnp.arange(seq, dtype=F32)[:, None] * inv[None, :]
    cos, sin = jnp.cos(ang), jnp.sin(ang)
    one = jnp.ones((seq, D_NOPE), F32)
    zero_n = jnp.zeros((seq, D_NOPE), F32)
    zero_p = jnp.zeros((seq, D_HEAD_PAD - D_NOPE - D_ROPE), F32)
    q_scale = (D_NOPE + D_ROPE) ** -0.5 * LOG2E
    cq = jnp.concatenate([one, cos, cos, zero_p], axis=1) * q_scale
    sq = jnp.concatenate([zero_n, sin, sin, zero_p], axis=1) * q_scale
    ck = jnp.concatenate([zero_n, cos, cos, zero_p], axis=1)
    sk = jnp.concatenate([zero_n, sin, sin, zero_p], axis=1)
    return cq.T, sq.T, ck, sk


def _trunk(x, ln_in, layers):
    batch, seq, _ = x.shape
    tabs = _rope_tables(seq)
    xf = x.reshape(batch * seq, D_MODEL)
    for l, lw in enumerate(layers):
        outs = _in_proj(xf, lw, tabs, seq, ln=ln_in if l == 0 else None)
        if l == 0:
            xf, outs = outs[0], outs[1:]
        q, k, vt, xp, km, om, qmt, vmt, gt = outs
        yat = _attention(q, k, vt, batch, seq, q_tiles=4 if seq >= 4096 else 2)
        yp = _pool(xp, lw["w_pool"], lw["pool_scale"], batch, seq)
        a_f, a_b, aux, u = _mlstm_intra(qmt, km, vmt, gt)
        hf, hb = _mlstm_scan(qmt, a_f, a_b, aux, u, batch, seq)
        xf = _out_ffn(xf, yat, yp, hf, hb, om, lw)
    return xf.reshape(batch, seq, D_MODEL)


def kernel(x_prompt, x_sample, ln_in_g, ln_in_b, w_in, q_norm_g, w_uq, kv_norm_g, w_ukv, w_pool, pool_scale,
           mlstm_gate_bias, mlstm_norm_g, w_out, ln1_g, ln1_b, w_gate, w_up, w_down, ln2_g, ln2_b):
    layers = [_prep_layer_weights(l, w_in, q_norm_g, w_uq, kv_norm_g, w_ukv, w_pool, pool_scale,
                                  mlstm_gate_bias, mlstm_norm_g, w_out, ln1_g, ln1_b, w_gate, w_up, w_down,
                                  ln2_g, ln2_b) for l in range(DEPTH)]
    ln_in = (ln_in_g.reshape(1, -1), ln_in_b.reshape(1, -1))
    return (_trunk(x_prompt, ln_in, layers), _trunk(x_sample, ln_in, layers))
```
